```python
import math
import jax, jax.numpy as jnp
from jax import lax
import numpy as np

D_MODEL = 4096
BATCH = 4
SEQ = 2048
DEPTH = 4
DEC_BATCH = 8
DEC_SEQ = 4
PAST_LEN = 8192
PAGE_SIZE = 128

N_MIXERS = 2
N_FOX = (DEPTH + N_MIXERS - 1) // N_MIXERS
N_SSD = DEPTH // N_MIXERS
EPS = 1e-6
FOX_HEADS = 32
FOX_HEAD_DIM = D_MODEL // FOX_HEADS
FOX_WIDTH = FOX_HEADS * FOX_HEAD_DIM
Q_BLOCK = 128
FORGET_BIAS_MEAN = 4.0
SSD_EXPAND = 2
SSD_INNER = SSD_EXPAND * D_MODEL
SSD_HEAD_DIM = 64
SSD_HEADS = SSD_INNER // SSD_HEAD_DIM
SSD_GROUPS = 8
SSD_STATE = 128
SSD_CONV = 4
SSD_CONV_DIM = SSD_INNER + 2 * SSD_GROUPS * SSD_STATE
SSD_CHUNK = 128

kernel_name = 'fox_mamba2_adaln_hybrid_step'


def rms_norm(x, g):
    xf = x.astype(jnp.float32)
    y = xf * lax.rsqrt(jnp.mean(xf * xf, axis=-1, keepdims=True) + EPS)
    return (y * g.astype(jnp.float32)).astype(x.dtype)


def modulate(x, c, g, w_ada, b_ada):
    mod = jax.nn.silu(c) @ w_ada + b_ada
    shift, scale, gate = jnp.split(mod, 3, axis=-1)
    h = rms_norm(x, g) * (1 + scale[:, None]) + shift[:, None]
    return h, gate[:, None]


def gather_pages(pool, page_table):
    g = pool[page_table]
    return g.reshape((page_table.shape[0], -1) + pool.shape[2:])


def fox_project(h, w_in, b_f):
    bt, l, _ = h.shape
    q, k, v, z, f_logit = jnp.split(
        h @ w_in, [FOX_WIDTH, 2 * FOX_WIDTH, 3 * FOX_WIDTH, 4 * FOX_WIDTH], axis=-1)
    shp = (bt, l, FOX_HEADS, FOX_HEAD_DIM)
    logf = jax.nn.log_sigmoid((f_logit + b_f).astype(jnp.float32))
    return q.reshape(shp), k.reshape(shp), v.reshape(shp), z, logf


def fox_attend(q, k, v, f_q, f_k, q_pos):
    s = jnp.einsum('bqhd,bkhd->bhqk', q, k).astype(jnp.float32) * (FOX_HEAD_DIM ** -0.5)
    s = s + jnp.swapaxes(f_q, 1, 2)[..., None] - jnp.swapaxes(f_k, 1, 2)[:, :, None, :]
    mask = jnp.arange(k.shape[1])[None, :] <= q_pos[:, None]
    s = jnp.where(mask[None, None], s, -jnp.inf)
    p = jax.nn.softmax(s, axis=-1)
    return jnp.einsum('bhqk,bkhd->bqhd', p.astype(v.dtype), v)


def fox_output(o, z, w_out):
    bt, l = o.shape[:2]
    return (o.reshape(bt, l, FOX_WIDTH) * jax.nn.silu(z)) @ w_out


def fox_prompt(h, w_in, b_f, w_out):
    q, k, v, z, logf = fox_project(h, w_in, b_f)
    bt, l = h.shape[:2]
    f_cum = jnp.cumsum(logf, axis=1)
    nb = l // Q_BLOCK
    qb = q.reshape(bt, nb, Q_BLOCK, FOX_HEADS, FOX_HEAD_DIM).swapaxes(0, 1)
    fb = f_cum.reshape(bt, nb, Q_BLOCK, FOX_HEADS).swapaxes(0, 1)
    pb = jnp.arange(l).reshape(nb, Q_BLOCK)

    def block(args):
        q_blk, f_blk, p_blk = args
        return fox_attend(q_blk, k, v, f_blk, f_cum, p_blk)

    o = lax.map(block, (qb, fb, pb)).swapaxes(0, 1).reshape(bt, l, FOX_HEADS, FOX_HEAD_DIM)
    return fox_output(o, z, w_out), k, v, logf


def fox_sample(h, k_past, v_past, logf_past, w_in, b_f, w_out):
    q, k, v, z, logf = fox_project(h, w_in, b_f)
    past = k_past.shape[1]
    k_all = jnp.concatenate([k_past.astype(k.dtype), k], axis=1)
    v_all = jnp.concatenate([v_past.astype(v.dtype), v], axis=1)
    f_cum = jnp.cumsum(jnp.concatenate([logf_past.astype(jnp.float32), logf], axis=1), axis=1)
    q_pos = past + jnp.arange(h.shape[1])
    o = fox_attend(q, k_all, v_all, f_cum[:, past:], f_cum, q_pos)
    return fox_output(o, z, w_out), k, v, logf


def ssd_scan(x, dt, a, b_in, c_in, state0):
    bt, l, nh, p = x.shape
    g, n = b_in.shape[2], b_in.shape[3]
    r = nh // g
    chunk = min(SSD_CHUNK, l)
    pad = (-l) % chunk
    nc = (l + pad) // chunk

    def chunks(t):
        t = jnp.pad(t, [(0, 0), (0, pad)] + [(0, 0)] * (t.ndim - 2))
        return t.reshape((bt, nc, chunk) + t.shape[2:]).swapaxes(0, 1)

    xs = chunks(x.reshape(bt, l, g, r, p))
    dts = chunks(dt.reshape(bt, l, g, r))
    bs = chunks(b_in)
    cs = chunks(c_in)
    a_gr = a.reshape(g, r)
    causal = jnp.tril(jnp.ones((chunk, chunk), dtype=bool))

    def step(state, inp):
        xc, dtc, bc, cc = inp
        acum = jnp.cumsum(dtc * a_gr, axis=1)
        seg = acum[:, :, None] - acum[:, None, :]
        decay = jnp.exp(jnp.where(causal[None, :, :, None, None], seg, -jnp.inf))
        xdt = xc * dtc[..., None]
        cb = jnp.einsum('blgn,bsgn->blsg', cc, bc)
        y = jnp.einsum('blsg,blsgr,bsgrp->blgrp', cb, decay, xdt)
        y = y + jnp.einsum('blgn,bgrpn->blgrp', cc, state) * jnp.exp(acum)[..., None]
        to_end = jnp.exp(acum[:, -1:] - acum)
        new_state = (state * jnp.exp(acum[:, -1])[..., None, None]
                     + jnp.einsum('bsgn,bsgr,bsgrp->bgrpn', bc, to_end, xdt))
        return new_state, y

    state, ys = lax.scan(step, state0.reshape(bt, g, r, p, n).astype(jnp.float32),
                         (xs, dts, bs, cs))
    y = ys.swapaxes(0, 1).reshape(bt, nc * chunk, nh, p)[:, :l]
    return y, state.reshape(bt, nh, p, n)


def ssd_mixer(h, conv_prev, ssm_prev, w_in, conv_w, conv_b, dt_bias, a_log, d_skip, norm_w, w_out):
    bt, l, _ = h.shape
    z, xbc, dt = jnp.split(h @ w_in, [SSD_INNER, SSD_INNER + SSD_CONV_DIM], axis=-1)
    xbc_full = jnp.concatenate([conv_prev.astype(xbc.dtype), xbc], axis=1)
    conv = conv_b + sum(xbc_full[:, i:i + l] * conv_w[i] for i in range(SSD_CONV))
    xbc_act = jax.nn.silu(conv)
    new_conv = xbc_full[:, l:]
    xs, bs, cs = jnp.split(xbc_act, [SSD_INNER, SSD_INNER + SSD_GROUPS * SSD_STATE], axis=-1)
    x = xs.reshape(bt, l, SSD_HEADS, SSD_HEAD_DIM)
    dt = jax.nn.softplus((dt + dt_bias).astype(jnp.float32))
    a = -jnp.exp(a_log.astype(jnp.float32))
    y, new_ssm = ssd_scan(x, dt, a, bs.reshape(bt, l, SSD_GROUPS, SSD_STATE),
                          cs.reshape(bt, l, SSD_GROUPS, SSD_STATE), ssm_prev)
    y = y + d_skip[:, None] * x
    y = (y.reshape(bt, l, SSD_INNER) * jax.nn.silu(z)).astype(h.dtype)
    y = rms_norm(y.reshape(bt, l, SSD_GROUPS, SSD_INNER // SSD_GROUPS),
                 norm_w.reshape(SSD_GROUPS, SSD_INNER // SSD_GROUPS)).reshape(bt, l, SSD_INNER)
    return y @ w_out, new_conv, new_ssm


def setup_inputs(seed: int = 0) -> dict:
    key = jax.random.key(seed)
    ks = jax.random.split(key, 25)
    f32 = jnp.float32

    def nrm(k, shape, scale):
        return jax.random.normal(k, shape, f32) * scale

    n_pages = PAST_LEN // PAGE_SIZE
    n_used = DEC_BATCH * n_pages
    n_pool = n_used + n_used // 4
    n_in_ssd = 2 * SSD_INNER + 2 * SSD_GROUPS * SSD_STATE + SSD_HEADS
    dt0 = jnp.exp(jax.random.uniform(ks[19], (N_SSD, SSD_HEADS), f32, math.log(1e-3), math.log(1e-1)))
    return {
        'x_prompt': nrm(ks[0], (BATCH, SEQ, D_MODEL), 1.0),
        'x_sample': nrm(ks[1], (DEC_BATCH, DEC_SEQ, D_MODEL), 1.0),
        'c_prompt': nrm(ks[2], (BATCH, D_MODEL), 1.0),
        'c_sample': nrm(ks[3], (DEC_BATCH, D_MODEL), 1.0),
        'cache_k': nrm(ks[4], (N_FOX, n_pool, PAGE_SIZE, FOX_HEADS, FOX_HEAD_DIM), 1.0),
        'cache_v': nrm(ks[5], (N_FOX, n_pool, PAGE_SIZE, FOX_HEADS, FOX_HEAD_DIM), 1.0),
        'cache_logf': jax.nn.log_sigmoid(
            FORGET_BIAS_MEAN + nrm(ks[6], (N_FOX, n_pool, PAGE_SIZE, FOX_HEADS), 1.0)),
        'state_ssm': nrm(ks[7], (N_SSD, DEC_BATCH, SSD_HEADS, SSD_HEAD_DIM, SSD_STATE), 0.1),
        'state_conv': nrm(ks[8], (N_SSD, DEC_BATCH, SSD_CONV - 1, SSD_CONV_DIM), 1.0),
        'page_table': jax.random.permutation(ks[9], n_pool)[:n_used].reshape(
            DEC_BATCH, n_pages).astype(jnp.int32),
        'norm_w': 1.0 + nrm(ks[10], (DEPTH, D_MODEL), 0.02),
        'ada_w': nrm(ks[11], (DEPTH, D_MODEL, 3 * D_MODEL), D_MODEL ** -0.5),
        'ada_b': nrm(ks[12], (DEPTH, 3 * D_MODEL), 0.02),
        'fox_w_in': nrm(ks[13], (N_FOX, D_MODEL, 4 * FOX_WIDTH + FOX_HEADS), D_MODEL ** -0.5),
        'fox_b_f': FORGET_BIAS_MEAN + nrm(ks[14], (N_FOX, FOX_HEADS), 1.0),
        'fox_w_out': nrm(ks[15], (N_FOX, FOX_WIDTH, D_MODEL), FOX_WIDTH ** -0.5),
        'ssd_w_in': nrm(ks[16], (N_SSD, D_MODEL, n_in_ssd), D_MODEL ** -0.5),
        'ssd_conv_w': nrm(ks[17], (N_SSD, SSD_CONV, SSD_CONV_DIM), SSD_CONV ** -0.5),
        'ssd_conv_b': nrm(ks[18], (N_SSD, SSD_CONV_DIM), 0.02),
        'ssd_dt_bias': dt0 + jnp.log(-jnp.expm1(-dt0)),
        'ssd_a_log': jnp.log(jax.random.uniform(ks[20], (N_SSD, SSD_HEADS), f32, 1.0, 16.0)),
        'ssd_d': 1.0 + nrm(ks[21], (N_SSD, SSD_HEADS), 0.1),
        'ssd_norm_w': 1.0 + nrm(ks[22], (N_SSD, SSD_INNER), 0.02),
        'ssd_w_out': nrm(ks[23], (N_SSD, SSD_INNER, D_MODEL), SSD_INNER ** -0.5),
        'final_norm_w': 1.0 + nrm(ks[24], (D_MODEL,), 0.02),
    }


def reference(x_prompt, x_sample, c_prompt, c_sample, cache_k, cache_v, cache_logf,
              state_ssm, state_conv, page_table, norm_w, ada_w, ada_b,
              fox_w_in, fox_b_f, fox_w_out, ssd_w_in, ssd_conv_w, ssd_conv_b,
              ssd_dt_bias, ssd_a_log, ssd_d, ssd_norm_w, ssd_w_out, final_norm_w):
    xp, xs = x_prompt, x_sample
    bp = x_prompt.shape[0]
    kp_l, vp_l, fp_l, ks_l, vs_l, fs_l = [], [], [], [], [], []
    sp_l, cp_l, ss_l, cs_l = [], [], [], []
    for i in range(DEPTH):
        j = i // N_MIXERS
        hp, gp = modulate(xp, c_prompt, norm_w[i], ada_w[i], ada_b[i])
        hs, gs = modulate(xs, c_sample, norm_w[i], ada_w[i], ada_b[i])
        if i % N_MIXERS == 0:
            op, kp, vp, fp = fox_prompt(hp, fox_w_in[j], fox_b_f[j], fox_w_out[j])
            os_, ks_, vs_, fs_ = fox_sample(
                hs, gather_pages(cache_k[j], page_table), gather_pages(cache_v[j], page_table),
                gather_pages(cache_logf[j], page_table), fox_w_in[j], fox_b_f[j], fox_w_out[j])
            kp_l.append(kp); vp_l.append(vp); fp_l.append(fp)
            ks_l.append(ks_); vs_l.append(vs_); fs_l.append(fs_)
        else:
            w = (ssd_w_in[j], ssd_conv_w[j], ssd_conv_b[j], ssd_dt_bias[j], ssd_a_log[j],
                 ssd_d[j], ssd_norm_w[j], ssd_w_out[j])
            op, cvp, smp = ssd_mixer(
                hp, jnp.zeros((bp, SSD_CONV - 1, SSD_CONV_DIM), hp.dtype),
                jnp.zeros((bp, SSD_HEADS, SSD_HEAD_DIM, SSD_STATE), jnp.float32), *w)
            os_, cvs, sms = ssd_mixer(hs, state_conv[j], state_ssm[j], *w)
            sp_l.append(smp); cp_l.append(cvp); ss_l.append(sms); cs_l.append(cvs)
        xp = xp + gp * op
        xs = xs + gs * os_
    y_prompt = rms_norm(xp, final_norm_w)
    y_sample = rms_norm(xs, final_norm_w)
    return (y_prompt, y_sample,
            jnp.stack(kp_l), jnp.stack(vp_l), jnp.stack(fp_l),
            jnp.stack(ks_l), jnp.stack(vs_l), jnp.stack(fs_l),
            jnp.stack(sp_l), jnp.stack(cp_l), jnp.stack(ss_l), jnp.stack(cs_l))
```

```python
import functools

import jax
import jax.numpy as jnp
from jax import lax
from jax.experimental import pallas as pl
from jax.experimental.pallas import tpu as pltpu

F32 = jnp.float32
BF16 = jnp.bfloat16

EPS = 1e-6
LANES = 128
SUBLANES = 8
PAGE = 128
FOX_HEADS = 32
FOX_HEAD_DIM = 128
SSD_GROUPS = 8
SSD_HEAD_DIM = 64
SSD_STATE = 128
SSD_HEADS_PER_GROUP = 16
SSD_GROUP_WIDTH = SSD_HEADS_PER_GROUP * SSD_HEAD_DIM
SSD_CHUNK = 128
SSD_CONV = 4
V7X_VMEM_BYTES = 64 * 1024 * 1024
VMEM_LIMIT = V7X_VMEM_BYTES * 7 // 8


def _params(semantics):
    return pltpu.CompilerParams(dimension_semantics=semantics, vmem_limit_bytes=VMEM_LIMIT)


def _silu(x):
    return x * (1.0 / (1.0 + jnp.exp(-x)))


def _softplus(x):
    return jnp.maximum(x, 0.0) + jnp.log1p(jnp.exp(-jnp.abs(x)))


def _tri(n, lower):
    r = lax.broadcasted_iota(jnp.int32, (n, n), 0)
    c = lax.broadcasted_iota(jnp.int32, (n, n), 1)
    return ((r >= c) if lower else (r <= c)).astype(F32)


def _ada_body(c_ref, w_ref, b_ref, o_ref):
    a = _silu(c_ref[...]).astype(BF16)
    o_ref[...] = jnp.dot(a, w_ref[...].astype(BF16), preferred_element_type=F32) + b_ref[...]


def _ada_mod(c_all, ada_w, ada_b, tn=512):
    depth, d, n = ada_w.shape
    rows = c_all.shape[0]
    return pl.pallas_call(
        _ada_body,
        grid=(depth, n // tn),
        in_specs=[
            pl.BlockSpec((rows, d), lambda l, j: (0, 0)),
            pl.BlockSpec((None, d, tn), lambda l, j: (l, 0, j)),
            pl.BlockSpec((None, 1, tn), lambda l, j: (l, 0, j)),
        ],
        out_specs=pl.BlockSpec((None, rows, tn), lambda l, j: (l, 0, j)),
        out_shape=jax.ShapeDtypeStruct((depth, rows, n), F32),
        compiler_params=_params(("arbitrary", "arbitrary")),
        name="ada_mod",
    )(c_all, ada_w, ada_b.reshape(depth, 1, n))


def _norm_mod_body(x_ref, g_ref, sc_ref, sh_ref, o_ref):
    x = x_ref[...]
    y = x * lax.rsqrt(jnp.mean(x * x, axis=-1, keepdims=True) + EPS) * g_ref[...]
    o_ref[...] = (y * (1.0 + sc_ref[...]) + sh_ref[...]).astype(o_ref.dtype)


def _norm_body(x_ref, g_ref, o_ref):
    x = x_ref[...]
    o_ref[...] = (x * lax.rsqrt(jnp.mean(x * x, axis=-1, keepdims=True) + EPS) * g_ref[...]).astype(o_ref.dtype)


def _row_or_batch_spec(arr, tr, tn, rows_per_batch):
    if arr.ndim == 3:
        tiles_per_batch = rows_per_batch // tr
        return pl.BlockSpec((None, 1, tn), lambda i, j: (i // tiles_per_batch, 0, j))
    return pl.BlockSpec((tr, tn), lambda i, j: (i, j))


def _norm_mod(x, g, scale, shift, rows_per_batch, tr):
    m, d = x.shape
    return pl.pallas_call(
        _norm_mod_body,
        grid=(m // tr, 1),
        in_specs=[
            pl.BlockSpec((tr, d), lambda i, j: (i, 0)),
            pl.BlockSpec((1, d), lambda i, j: (0, 0)),
            _row_or_batch_spec(scale, tr, d, rows_per_batch),
            _row_or_batch_spec(shift, tr, d, rows_per_batch),
        ],
        out_specs=pl.BlockSpec((tr, d), lambda i, j: (i, 0)),
        out_shape=jax.ShapeDtypeStruct((m, d), BF16),
        compiler_params=_params(("arbitrary", "arbitrary")),
        name="norm_mod",
    )(x, g.reshape(1, d), scale, shift)


def _final_norm(x, g, tr):
    m, d = x.shape
    return pl.pallas_call(
        _norm_body,
        grid=(m // tr,),
        in_specs=[pl.BlockSpec((tr, d), lambda i: (i, 0)), pl.BlockSpec((1, d), lambda i: (0, 0))],
        out_specs=pl.BlockSpec((tr, d), lambda i: (i, 0)),
        out_shape=jax.ShapeDtypeStruct((m, d), F32),
        compiler_params=_params(("arbitrary",)),
        name="final_norm",
    )(x, g.reshape(1, d))


def _mm_plain_body(a_ref, w_ref, *o_refs):
    acc = jnp.dot(a_ref[...], w_ref[...], preferred_element_type=F32)
    for o_ref in o_refs:
        o_ref[...] = acc.astype(o_ref.dtype)


def _mm_logsig_body(a_ref, w_ref, b_ref, o_ref):
    acc = jnp.dot(a_ref[...], w_ref[...], preferred_element_type=F32) + b_ref[...]
    o_ref[...] = -_softplus(-acc)


def _mm_resid_body(a_ref, w_ref, x_ref, g_ref, o_ref):
    acc = jnp.dot(a_ref[...], w_ref[...], preferred_element_type=F32)
    o_ref[...] = x_ref[...] + g_ref[...] * acc


def _mm_specs(a, w, col0, tm, tn):
    k = a.shape[1]
    off = col0 // tn
    return [pl.BlockSpec((tm, k), lambda i, j: (i, 0)),
            pl.BlockSpec((k, tn), lambda i, j: (0, j + off))]


def _mm_plain(a, w, col0, n, out_dtypes, tm, tn, name):
    m = a.shape[0]
    return pl.pallas_call(
        _mm_plain_body,
        grid=(m // tm, n // tn),
        in_specs=_mm_specs(a, w, col0, tm, tn),
        out_specs=[pl.BlockSpec((tm, tn), lambda i, j: (i, j)) for _ in out_dtypes],
        out_shape=[jax.ShapeDtypeStruct((m, n), dt) for dt in out_dtypes],
        compiler_params=_params(("arbitrary", "arbitrary")),
        name=name,
    )(a, w)


def _mm_logsig(a, w, bias, tm, name):
    m = a.shape[0]
    n = w.shape[1]
    return pl.pallas_call(
        _mm_logsig_body,
        grid=(m // tm, 1),
        in_specs=_mm_specs(a, w, 0, tm, n) + [pl.BlockSpec((1, n), lambda i, j: (0, 0))],
        out_specs=pl.BlockSpec((tm, n), lambda i, j: (i, 0)),
        out_shape=jax.ShapeDtypeStruct((m, n), F32),
        compiler_params=_params(("arbitrary", "arbitrary")),
        name=name,
    )(a, w, bias)


def _mm_resid(a, w, x, gate, rows_per_batch, tm, tn, name):
    m, n = x.shape
    return pl.pallas_call(
        _mm_resid_body,
        grid=(m // tm, n // tn),
        in_specs=_mm_specs(a, w, 0, tm, tn) + [
            pl.BlockSpec((tm, tn), lambda i, j: (i, j)),
            _row_or_batch_spec(gate, tm, tn, rows_per_batch),
        ],
        out_specs=pl.BlockSpec((tm, tn), lambda i, j: (i, j)),
        out_shape=jax.ShapeDtypeStruct((m, n), F32),
        compiler_params=_params(("arbitrary", "arbitrary")),
        name=name,
    )(a, w, x, gate)


def _cumsum_body(tbl_ref, x_ref, tail_ref, f_ref, ftail_ref, carry_ref):
    del tbl_ref
    p = pl.program_id(1)

    @pl.when(p == 0)
    def _():
        carry_ref[...] = jnp.zeros_like(carry_ref)

    f = jnp.dot(_tri(PAGE, True), x_ref[...], precision=lax.Precision.HIGHEST,
                preferred_element_type=F32) + carry_ref[0:1, :]
    f_ref[...] = f
    carry_ref[...] = jnp.broadcast_to(f[PAGE - 1:PAGE, :], carry_ref.shape)

    @pl.when(p == pl.num_programs(1) - 1)
    def _():
        tail = tail_ref[...]
        row = lax.broadcasted_iota(jnp.int32, tail.shape, 0)
        acc = carry_ref[...]
        for k in range(SUBLANES):
            acc = acc + jnp.where(row >= k, tail[k:k + 1, :], 0.0)
        ftail_ref[...] = acc


def _paged_cumsum(pool, table, tail):
    b, n_pages = table.shape
    h = pool.shape[-1]
    return pl.pallas_call(
        _cumsum_body,
        grid_spec=pltpu.PrefetchScalarGridSpec(
            num_scalar_prefetch=1,
            grid=(b, n_pages),
            in_specs=[
                pl.BlockSpec((None, PAGE, h), lambda i, p, tbl: (tbl[i, p], 0, 0)),
                pl.BlockSpec((None, SUBLANES, h), lambda i, p, tbl: (i, 0, 0)),
            ],
            out_specs=[
                pl.BlockSpec((None, PAGE, h), lambda i, p, tbl: (i, p, 0)),
                pl.BlockSpec((None, SUBLANES, h), lambda i, p, tbl: (i, 0, 0)),
            ],
            scratch_shapes=[pltpu.VMEM((SUBLANES, h), F32)],
        ),
        out_shape=[jax.ShapeDtypeStruct((b, n_pages * PAGE, h), F32),
                   jax.ShapeDtypeStruct((b, SUBLANES, h), F32)],
        compiler_params=_params(("arbitrary", "arbitrary")),
        name="logf_cumsum",
    )(table, pool, tail)


def _fox_prompt_body(q_ref, k_ref, v_ref, z_ref, f_ref, o_ref, m_ref, l_ref, acc_ref, fq_ref, *, tq, tk):
    t = q_ref.shape[0]
    scale = FOX_HEAD_DIM ** -0.5
    nrep = tk // LANES
    for qi in range(t // tq):
        q0 = qi * tq
        q = q_ref[q0:q0 + tq, :]
        fq_ref[...] = jnp.broadcast_to(f_ref[qi], (LANES, tq)).T
        m_ref[...] = jnp.full_like(m_ref, -jnp.inf)
        l_ref[...] = jnp.zeros_like(l_ref)
        acc_ref[...] = jnp.zeros_like(acc_ref)

        def kv_step(j, k0, masked):
            kb = k_ref[pl.ds(k0, tk), :]
            vb = v_ref[pl.ds(k0, tk), :]
            s = lax.dot_general(q, kb, (((1,), (1,)), ((), ())), preferred_element_type=F32) * scale
            s = s + (pltpu.repeat(fq_ref[...], nrep, 1) - f_ref[j])
            if masked:
                r = lax.broadcasted_iota(jnp.int32, (tq, tk), 0)
                c = lax.broadcasted_iota(jnp.int32, (tq, tk), 1)
                s = jnp.where(c <= r, s, -jnp.inf)
            m_prev = m_ref[...]
            m_next = jnp.maximum(m_prev, jnp.max(s, axis=1, keepdims=True))
            p = jnp.exp(s - pltpu.repeat(m_next, nrep, 1))
            alpha = jnp.exp(m_prev - m_next)
            l_ref[...] = alpha * l_ref[...] + jnp.sum(p, axis=1, keepdims=True)
            acc_ref[...] = alpha * acc_ref[...] + jnp.dot(p.astype(BF16), vb, preferred_element_type=F32)
            m_ref[...] = m_next

        def off_diag(j, carry):
            kv_step(j, pl.multiple_of(j * tk, tk), False)
            return carry

        if qi > 0:
            lax.fori_loop(0, qi, off_diag, 0)
        kv_step(qi, q0, True)
        o = acc_ref[...] / l_ref[...]
        zb = z_ref[q0:q0 + tq, :].astype(F32)
        o_ref[q0:q0 + tq, :] = (o * _silu(zb)).astype(o_ref.dtype)


def _fox_prompt_attn(q, k, v, z, f_t, batch, t, tq=512, tk=512):
    assert tq == tk and t % tq == 0
    hd = FOX_HEAD_DIM
    nk = t // tk
    f_t = f_t.reshape(batch, FOX_HEADS, nk, 1, tk)
    tb = t
    body = functools.partial(_fox_prompt_body, tq=tq, tk=tk)
    row_spec = pl.BlockSpec((tb, hd), lambda b, h: (b, h))
    return pl.pallas_call(
        body,
        grid=(batch, FOX_HEADS),
        in_specs=[row_spec, row_spec, row_spec, row_spec,
                  pl.BlockSpec((None, None, nk, 1, tk), lambda b, h: (b, h, 0, 0, 0))],
        out_specs=row_spec,
        out_shape=jax.ShapeDtypeStruct(q.shape, BF16),
        scratch_shapes=[pltpu.VMEM((tq, LANES), F32), pltpu.VMEM((tq, LANES), F32),
                        pltpu.VMEM((tq, hd), F32), pltpu.VMEM((tq, LANES), F32)],
        compiler_params=_params(("arbitrary", "arbitrary")),
        name="fox_prompt_attn",
    )(q, k, v, z, f_t)


QPAD = 16


def _fox_decode_body(tbl_ref, q_ref, k_ref, v_ref, fk_ref, fq_ref, kn_ref, vn_ref, fkn_ref, z_ref, o_ref,
                     m_ref, l_ref, acc_ref, *, n_new):
    del tbl_ref
    p = pl.program_id(1)
    scale = FOX_HEAD_DIM ** -0.5
    hd = FOX_HEAD_DIM
    rows = FOX_HEADS * QPAD

    @pl.when(p == 0)
    def _():
        m_ref[...] = jnp.full_like(m_ref, -jnp.inf)
        l_ref[...] = jnp.zeros_like(l_ref)
        acc_ref[...] = jnp.zeros_like(acc_ref)

    def update(load_k, load_v, fk, masked):
        s_parts = []
        for h in range(FOX_HEADS):
            qh = q_ref[:, h * hd:(h + 1) * hd]
            s_h = lax.dot_general(qh, load_k(h), (((1,), (1,)), ((), ())), preferred_element_type=F32)
            s_parts.append(s_h * scale - jnp.broadcast_to(fk[h:h + 1, :], (QPAD, PAGE)))
        s = jnp.concatenate(s_parts, axis=0) + fq_ref[...]
        if masked:
            r = lax.broadcasted_iota(jnp.int32, (rows, PAGE), 0) & (QPAD - 1)
            c = lax.broadcasted_iota(jnp.int32, (rows, PAGE), 1)
            s = jnp.where((c <= r) & (c < n_new), s, -jnp.inf)
        m_prev = m_ref[...]
        m_next = jnp.maximum(m_prev, jnp.max(s, axis=1, keepdims=True))
        pr = jnp.exp(s - m_next)
        alpha = jnp.exp(m_prev - m_next)
        l_ref[...] = alpha * l_ref[...] + jnp.sum(pr, axis=1, keepdims=True)
        pb = pr.astype(BF16)
        pv = [jnp.dot(pb[h * QPAD:(h + 1) * QPAD, :], load_v(h), preferred_element_type=F32)
              for h in range(FOX_HEADS)]
        acc_ref[...] = alpha * acc_ref[...] + jnp.concatenate(pv, axis=0)
        m_ref[...] = m_next

    update(lambda h: k_ref[:, h * hd:(h + 1) * hd].astype(BF16),
           lambda h: v_ref[:, h * hd:(h + 1) * hd].astype(BF16),
           fk_ref[...], False)

    @pl.when(p == pl.num_programs(1) - 1)
    def _():
        update(lambda h: kn_ref[:, h * hd:(h + 1) * hd],
               lambda h: vn_ref[:, h * hd:(h + 1) * hd],
               fkn_ref[...], True)
        o = acc_ref[...] / l_ref[...]
        for h in range(FOX_HEADS):
            zb = z_ref[:, h * hd:(h + 1) * hd].astype(F32)
            o_ref[:, h * hd:(h + 1) * hd] = (o[h * QPAD:(h + 1) * QPAD, :] * _silu(zb)).astype(o_ref.dtype)


def _fox_decode_attn(table, q8, k_pool, v_pool, fk_t, fq_rep, k_new, v_new, fk_new_t, z8, n_new):
    b, n_pages = table.shape
    width = FOX_HEADS * FOX_HEAD_DIM
    rows = FOX_HEADS * QPAD
    body = functools.partial(_fox_decode_body, n_new=n_new)
    per_b = lambda shape: pl.BlockSpec((None,) + shape, lambda i, p, tbl: (i, 0, 0))
    page = pl.BlockSpec((None, PAGE, width), lambda i, p, tbl: (tbl[i, p], 0, 0))
    return pl.pallas_call(
        body,
        grid_spec=pltpu.PrefetchScalarGridSpec(
            num_scalar_prefetch=1,
            grid=(b, n_pages),
            in_specs=[
                per_b((QPAD, width)), page, page,
                pl.BlockSpec((None, FOX_HEADS, PAGE), lambda i, p, tbl: (i, 0, p)),
                per_b((rows, LANES)),
                per_b((PAGE, width)), per_b((PAGE, width)),
                per_b((FOX_HEADS, PAGE)),
                per_b((QPAD, width)),
            ],
            out_specs=per_b((QPAD, width)),
            scratch_shapes=[pltpu.VMEM((rows, LANES), F32), pltpu.VMEM((rows, LANES), F32),
                            pltpu.VMEM((rows, FOX_HEAD_DIM), F32)],
        ),
        out_shape=jax.ShapeDtypeStruct((b, QPAD, width), BF16),
        compiler_params=_params(("arbitrary", "arbitrary")),
        name="fox_decode_attn",
    )(table, q8, k_pool, v_pool, fk_t, fq_rep, k_new, v_new, fk_new_t, z8)


def _conv_body(x_ref, init_ref, w_ref, b_ref, o_ref, carry_ref):
    @pl.when(pl.program_id(2) == 0)
    def _():
        carry_ref[...] = init_ref[...]

    x = x_ref[...].astype(F32)
    tt = x.shape[0]
    c8 = carry_ref[...]
    row = lax.broadcasted_iota(jnp.int32, c8.shape, 0)
    acc = b_ref[...] + x * w_ref[SSD_CONV - 1:SSD_CONV, :]
    for shift in range(1, SSD_CONV):
        rolled = pltpu.roll(x, shift, 0)
        top = jnp.where(row < shift, pltpu.roll(c8, shift, 0), rolled[0:SUBLANES, :])
        shifted = jnp.concatenate([top, rolled[SUBLANES:, :]], axis=0)
        acc = acc + shifted * w_ref[SSD_CONV - 1 - shift:SSD_CONV - shift, :]
    o_ref[...] = _silu(acc).astype(o_ref.dtype)
    carry_ref[...] = x[tt - SUBLANES:, :]


def _conv_silu(x, init, w, bias, batch, t, tt, tc=2048):
    c = x.shape[1]
    nt = t // tt
    return pl.pallas_call(
        _conv_body,
        grid=(batch, c // tc, nt),
        in_specs=[
            pl.BlockSpec((tt, tc), lambda b, j, i: (b * nt + i, j)),
            pl.BlockSpec((None, SUBLANES, tc), lambda b, j, i: (b, 0, j)),
            pl.BlockSpec((SSD_CONV, tc), lambda b, j, i: (0, j)),
            pl.BlockSpec((1, tc), lambda b, j, i: (0, j)),
        ],
        out_specs=pl.BlockSpec((tt, tc), lambda b, j, i: (b * nt + i, j)),
        out_shape=jax.ShapeDtypeStruct(x.shape, BF16),
        scratch_shapes=[pltpu.VMEM((SUBLANES, tc), F32)],
        compiler_params=_params(("arbitrary", "arbitrary", "arbitrary")),
        name="ssd_conv_silu",
    )(x, init, w, bias.reshape(1, c))


def _ssd_body(*refs, valid, has_init):
    if has_init:
        (x_ref, b_ref, c_ref, dt_ref, dtt_ref, z_ref, bias_ref, biast_ref, alog_ref, alogt_ref,
         d_ref, nw_ref, s0_ref, y_ref, sout_ref, st_ref, yacc_ref) = refs
    else:
        (x_ref, b_ref, c_ref, dt_ref, dtt_ref, z_ref, bias_ref, biast_ref, alog_ref, alogt_ref,
         d_ref, nw_ref, y_ref, sout_ref, st_ref, yacc_ref) = refs
        s0_ref = None
    ci = pl.program_id(2)
    L = SSD_CHUNK
    npairs = SSD_HEADS_PER_GROUP // 2

    @pl.when(ci == 0)
    def _():
        if has_init:
            for k in range(npairs):
                st_ref[:, k * LANES:(k + 1) * LANES] = s0_ref[k * LANES:(k + 1) * LANES, :].T
        else:
            st_ref[...] = jnp.zeros_like(st_ref)

    dt = _softplus(dt_ref[...] + bias_ref[...])
    dtt = _softplus(dtt_ref[...] + biast_ref[...])
    if valid < L:
        dt = jnp.where(lax.broadcasted_iota(jnp.int32, dt.shape, 0) < valid, dt, 0.0)
        dtt = jnp.where(lax.broadcasted_iota(jnp.int32, dtt.shape, 1) < valid, dtt, 0.0)
    acum = jnp.dot(_tri(L, True), dt * (-jnp.exp(alog_ref[...])),
                   precision=lax.Precision.HIGHEST, preferred_element_type=F32)
    acumt = jnp.dot(dtt * (-jnp.exp(alogt_ref[...])), _tri(L, False),
                    precision=lax.Precision.HIGHEST, preferred_element_type=F32)
    last_t = acumt[:, L - 1:L]
    wt = jnp.exp(last_t - acumt) * dtt
    chunk_decay = jnp.broadcast_to(jnp.exp(last_t), (SSD_HEADS_PER_GROUP, LANES))

    bm = b_ref[...]
    cm = c_ref[...]
    cb = lax.dot_general(cm, bm, (((1,), (1,)), ((), ())), preferred_element_type=F32)
    cm32 = cm.astype(F32)
    bmt = bm.astype(F32).T
    causal = lax.broadcasted_iota(jnp.int32, (L, L), 0) >= lax.broadcasted_iota(jnp.int32, (L, L), 1)
    lane = lax.broadcasted_iota(jnp.int32, (L, LANES), 1)
    first_head = lane < SSD_HEAD_DIM
    first_head_row = lax.broadcasted_iota(jnp.int32, (1, LANES), 1) < SSD_HEAD_DIM

    for k in range(npairs):
        xp = x_ref[:, k * LANES:(k + 1) * LANES]
        sp = st_ref[:, k * LANES:(k + 1) * LANES]
        rhs = jnp.concatenate([xp, sp.astype(BF16)], axis=0)
        ys, ups = [], []
        for r in (2 * k, 2 * k + 1):
            a_col = jnp.broadcast_to(acum[:, r:r + 1], (L, L))
            a_row = acumt[r:r + 1, :]
            dec = jnp.where(causal, jnp.exp(a_col - a_row), 0.0)
            m = cb * dec * dtt[r:r + 1, :]
            ec = cm32 * jnp.exp(a_col)
            lhs = jnp.concatenate([m, ec], axis=1).astype(BF16)
            ys.append(jnp.dot(lhs, rhs, preferred_element_type=F32))
            bw = (bmt * wt[r:r + 1, :]).astype(BF16)
            ups.append(jnp.dot(bw, xp, preferred_element_type=F32))
        yacc_ref[:, k * LANES:(k + 1) * LANES] = jnp.where(first_head, ys[0], ys[1])
        dec_row = jnp.where(first_head_row, chunk_decay[2 * k:2 * k + 1, :],
                            chunk_decay[2 * k + 1:2 * k + 2, :])
        st_ref[:, k * LANES:(k + 1) * LANES] = sp * dec_row + jnp.where(first_head, ups[0], ups[1])

    y = yacc_ref[...] + d_ref[...] * x_ref[...].astype(F32)
    y = y * _silu(z_ref[...].astype(F32))
    y = y * lax.rsqrt(jnp.mean(y * y, axis=-1, keepdims=True) + EPS) * nw_ref[...]
    y_ref[...] = y.astype(y_ref.dtype)

    @pl.when(ci == pl.num_programs(2) - 1)
    def _():
        for k in range(npairs):
            sout_ref[k * LANES:(k + 1) * LANES, :] = st_ref[:, k * LANES:(k + 1) * LANES].T


def _ssd_scan(act, dt_g, dtt_g, z, dt_bias, a_log, d_skip, norm_w, state0, batch, t, valid):
    L = SSD_CHUNK
    nc = t // L
    gw = SSD_GROUP_WIDTH
    hpg = SSD_HEADS_PER_GROUP
    g = SSD_GROUPS
    inner = g * gw
    b_blk0 = inner // SSD_STATE
    c_blk0 = b_blk0 + g
    has_init = state0 is not None
    body = functools.partial(_ssd_body, valid=valid, has_init=has_init)
    grp = lambda shape: pl.BlockSpec((None,) + shape, lambda b, gi, c: (gi, 0, 0))
    in_specs = [
        pl.BlockSpec((L, gw), lambda b, gi, c: (b * nc + c, gi)),
        pl.BlockSpec((L, SSD_STATE), lambda b, gi, c: (b * nc + c, b_blk0 + gi)),
        pl.BlockSpec((L, SSD_STATE), lambda b, gi, c: (b * nc + c, c_blk0 + gi)),
        pl.BlockSpec((None, L, hpg), lambda b, gi, c: (gi, b * nc + c, 0)),
        pl.BlockSpec((None, hpg, L), lambda b, gi, c: (gi, 0, b * nc + c)),
        pl.BlockSpec((L, gw), lambda b, gi, c: (b * nc + c, gi)),
        grp((1, hpg)), grp((hpg, 1)), grp((1, hpg)), grp((hpg, 1)),
        grp((1, gw)), grp((1, gw)),
    ]
    args = [act, act, act, dt_g, dtt_g, z,
            dt_bias.reshape(g, 1, hpg), dt_bias.reshape(g, hpg, 1),
            a_log.reshape(g, 1, hpg), a_log.reshape(g, hpg, 1),
            jnp.repeat(d_skip, SSD_HEAD_DIM).reshape(g, 1, gw), norm_w.reshape(g, 1, gw)]
    state_spec = pl.BlockSpec((None, gw, SSD_STATE), lambda b, gi, c: (b, gi, 0))
    if has_init:
        in_specs.append(state_spec)
        args.append(state0.reshape(batch, g * gw, SSD_STATE))
    y, s_out = pl.pallas_call(
        body,
        grid=(batch, g, nc),
        in_specs=in_specs,
        out_specs=[pl.BlockSpec((L, gw), lambda b, gi, c: (b * nc + c, gi)), state_spec],
        out_shape=[jax.ShapeDtypeStruct((batch * t, inner), BF16),
                   jax.ShapeDtypeStruct((batch, g * gw, SSD_STATE), F32)],
        scratch_shapes=[pltpu.VMEM((SSD_STATE, gw), F32), pltpu.VMEM((L, gw), F32)],
        compiler_params=_params(("arbitrary", "arbitrary", "arbitrary")),
        name="ssd_scan",
    )(*args)
    return y, s_out.reshape(batch, g * hpg, SSD_HEAD_DIM, SSD_STATE)


def _mod_parts(mod_l, nb_prompt, nb_sample, rows_per_sample, d):
    def prompt(a):
        return a[:nb_prompt].reshape(nb_prompt, 1, d)

    def sample(a):
        return jnp.repeat(a[nb_prompt:nb_prompt + nb_sample], rows_per_sample, axis=0)

    parts = [mod_l[:, i * d:(i + 1) * d] for i in range(3)]
    return [prompt(a) for a in parts], [sample(a) for a in parts]


def _pad_rows(a, batch, t, t_pad):
    c = a.shape[-1]
    a = a.reshape(batch, t, c)
    return jnp.pad(a, ((0, 0), (0, t_pad - t), (0, 0))).reshape(batch * t_pad, c)


def kernel(x_prompt, x_sample, c_prompt, c_sample, cache_k, cache_v, cache_logf, state_ssm, state_conv,
           page_table, norm_w, ada_w, ada_b, fox_w_in, fox_b_f, fox_w_out, ssd_w_in, ssd_conv_w, ssd_conv_b,
           ssd_dt_bias, ssd_a_log, ssd_d, ssd_norm_w, ssd_w_out, final_norm_w):
    bp, t, d = x_prompt.shape
    bs, ts, _ = x_sample.shape
    depth = norm_w.shape[0]
    mp, ms = bp * t, bs * ts
    width = FOX_HEADS * FOX_HEAD_DIM
    inner = SSD_GROUPS * SSD_GROUP_WIDTH
    conv_dim = inner + 2 * SSD_GROUPS * SSD_STATE
    n_pages = page_table.shape[1]
    tm = 512

    xp = x_prompt.reshape(mp, d)
    xs = x_sample.reshape(ms, d)

    c_all = jnp.concatenate([c_prompt, c_sample], axis=0)
    c_all = jnp.pad(c_all, ((0, (-c_all.shape[0]) % SUBLANES), (0, 0)))
    mod = _ada_mod(c_all, ada_w, ada_b)

    ident_table = jnp.arange(bp * (t // PAGE), dtype=jnp.int32).reshape(bp, t // PAGE)
    zero_tail_p = jnp.zeros((bp, SUBLANES, FOX_HEADS), F32)

    outs = {name: [] for name in ("kp", "vp", "fp", "ks", "vs", "fs", "sp", "cp", "ss", "cs")}
    for i in range(depth):
        j = i // 2
        (sh_p, sc_p, g_p), (sh_s, sc_s, g_s) = _mod_parts(mod[i], bp, bs, ts, d)
        hp = _norm_mod(xp, norm_w[i], sc_p, sh_p, t, 256)
        hs = _norm_mod(xs, norm_w[i], sc_s, sh_s, ts, ms)
        if i % 2 == 0:
            w_in = fox_w_in[j].astype(BF16)
            w_f = jnp.pad(fox_w_in[j][:, 4 * width:], ((0, 0), (0, LANES - FOX_HEADS))).astype(BF16)
            b_f = jnp.pad(fox_b_f[j], (0, LANES - FOX_HEADS)).reshape(1, LANES)
            w_out = fox_w_out[j].astype(BF16)

            def project(h, tmm):
                (q,) = _mm_plain(h, w_in, 0, width, [BF16], tmm, 512, "fox_q")
                k32, k16 = _mm_plain(h, w_in, width, width, [F32, BF16], tmm, 512, "fox_k")
                v32, v16 = _mm_plain(h, w_in, 2 * width, width, [F32, BF16], tmm, 512, "fox_v")
                (z,) = _mm_plain(h, w_in, 3 * width, width, [BF16], tmm, 512, "fox_z")
                logf = _mm_logsig(h, w_f, b_f, tmm, "fox_logf")[:, :FOX_HEADS]
                return q, k32, k16, v32, v16, z, logf

            q, k32, k16, v32, v16, z, logf = project(hp, tm)
            f_cum, _ = _paged_cumsum(logf.reshape(bp * (t // PAGE), PAGE, FOX_HEADS), ident_table, zero_tail_p)
            og = _fox_prompt_attn(q, k16, v16, z, jnp.swapaxes(f_cum, 1, 2), bp, t, tq=min(512, t), tk=min(512, t))
            xp = _mm_resid(og, w_out, xp, g_p, t, tm, 512, "fox_out")
            outs["kp"].append(k32.reshape(bp, t, FOX_HEADS, FOX_HEAD_DIM))
            outs["vp"].append(v32.reshape(bp, t, FOX_HEADS, FOX_HEAD_DIM))
            outs["fp"].append(logf.reshape(bp, t, FOX_HEADS))

            q, k32, k16, v32, v16, z, logf = project(hs, ms)
            tail = jnp.pad(logf.reshape(bs, ts, FOX_HEADS), ((0, 0), (0, SUBLANES - ts), (0, 0)))
            f_past, f_new = _paged_cumsum(cache_logf[j], page_table, tail)
            fk_t = jnp.swapaxes(f_past, 1, 2)
            f_new_t = jnp.swapaxes(f_new, 1, 2)
            fq_rep = jnp.pad(f_new_t, ((0, 0), (0, 0), (0, QPAD - SUBLANES))).reshape(bs, FOX_HEADS * QPAD, 1)
            fq_rep = jnp.broadcast_to(fq_rep, (bs, FOX_HEADS * QPAD, LANES))
            fk_new_t = jnp.pad(f_new_t, ((0, 0), (0, 0), (0, PAGE - SUBLANES)))

            def pad_q(a, rows):
                return jnp.pad(a.reshape(bs, ts, width), ((0, 0), (0, rows - ts), (0, 0)))

            og = _fox_decode_attn(page_table, pad_q(q, QPAD),
                                  cache_k[j].reshape(-1, PAGE, width), cache_v[j].reshape(-1, PAGE, width),
                                  fk_t, fq_rep, pad_q(k16, PAGE), pad_q(v16, PAGE), fk_new_t,
                                  pad_q(z, QPAD), ts)
            og = og[:, :ts].reshape(ms, width)
            xs = _mm_resid(og, w_out, xs, g_s, ts, ms, 512, "fox_out_s")
            outs["ks"].append(k32.reshape(bs, ts, FOX_HEADS, FOX_HEAD_DIM))
            outs["vs"].append(v32.reshape(bs, ts, FOX_HEADS, FOX_HEAD_DIM))
            outs["fs"].append(logf.reshape(bs, ts, FOX_HEADS))
        else:
            w_in = ssd_w_in[j].astype(BF16)
            w_out = ssd_w_out[j].astype(BF16)
            n_heads = SSD_GROUPS * SSD_HEADS_PER_GROUP

            def mixer(h, x_res, gate, batch, tt_, tmm, conv_prev, state0, suffix):
                m = batch * tt_
                (zz,) = _mm_plain(h, w_in, 0, inner, [BF16], tmm, 512, "ssd_z" + suffix)
                (xbc,) = _mm_plain(h, w_in, inner, conv_dim, [F32], tmm, 512, "ssd_xbc" + suffix)
                (dt_raw,) = _mm_plain(h, w_in, inner + conv_dim, n_heads, [F32], tmm, n_heads, "ssd_dt" + suffix)
                t_pad = -(-tt_ // SSD_CHUNK) * SSD_CHUNK
                if t_pad != tt_:
                    zz_p, xbc_p, dt_p = (_pad_rows(a, batch, tt_, t_pad) for a in (zz, xbc, dt_raw))
                else:
                    zz_p, xbc_p, dt_p = zz, xbc, dt_raw
                if conv_prev is None:
                    init = jnp.zeros((batch, SUBLANES, conv_dim), F32)
                    new_conv = xbc.reshape(batch, tt_, conv_dim)[:, tt_ - (SSD_CONV - 1):]
                else:
                    init = jnp.pad(conv_prev, ((0, 0), (SUBLANES - (SSD_CONV - 1), 0), (0, 0)))
                    full = jnp.concatenate([conv_prev, xbc.reshape(batch, tt_, conv_dim)], axis=1)
                    new_conv = full[:, tt_:]
                act = _conv_silu(xbc_p, init, ssd_conv_w[j], ssd_conv_b[j], batch, t_pad, SSD_CHUNK)
                dt_g = dt_p.reshape(batch * t_pad, SSD_GROUPS, SSD_HEADS_PER_GROUP).transpose(1, 0, 2)
                dtt_g = dt_g.transpose(0, 2, 1)
                y, new_state = _ssd_scan(act, dt_g, dtt_g, zz_p, ssd_dt_bias[j], ssd_a_log[j], ssd_d[j],
                                         ssd_norm_w[j], state0, batch, t_pad, min(tt_, SSD_CHUNK))
                if t_pad != tt_:
                    y = y.reshape(batch, t_pad, inner)[:, :tt_].reshape(m, inner)
                x_new = _mm_resid(y, w_out, x_res, gate, tt_, tmm, 512, "ssd_out" + suffix)
                return x_new, new_conv, new_state

            xp, cvp, smp = mixer(hp, xp, g_p, bp, t, tm, None, None, "")
            xs, cvs, sms = mixer(hs, xs, g_s, bs, ts, ms, state_conv[j], state_ssm[j], "_s")
            outs["sp"].append(smp)
            outs["cp"].append(cvp)
            outs["ss"].append(sms)
            outs["cs"].append(cvs)

    y_prompt = _final_norm(xp, final_norm_w, 256).reshape(bp, t, d)
    y_sample = _final_norm(xs, final_norm_w, ms).reshape(bs, ts, d)
    return (y_prompt, y_sample,
            jnp.stack(outs["kp"]), jnp.stack(outs["vp"]), jnp.stack(outs["fp"]),
            jnp.stack(outs["ks"]), jnp.stack(outs["vs"]), jnp.stack(outs["fs"]),
            jnp.stack(outs["sp"]), jnp.stack(outs["cp"]), jnp.stack(outs["ss"]), jnp.stack(outs["cs"]))
```

```python
import functools

import jax
import jax.numpy as jnp
from jax import lax
from jax.experimental import pallas as pl
from jax.experimental.pallas import tpu as pltpu

F32 = jnp.float32
BF16 = jnp.bfloat16

EPS = 1e-6
LANES = 128
SUBLANES = 8
PAGE = 128
FOX_HEADS = 32
FOX_HEAD_DIM = 128
SSD_GROUPS = 8
SSD_HEAD_DIM = 64
SSD_STATE = 128
SSD_HEADS_PER_GROUP = 16
SSD_GROUP_WIDTH = SSD_HEADS_PER_GROUP * SSD_HEAD_DIM
SSD_CHUNK = 128
SSD_CONV = 4
V7X_VMEM_BYTES = 64 * 1024 * 1024
VMEM_LIMIT = V7X_VMEM_BYTES * 7 // 8


def _params(semantics):
    return pltpu.CompilerParams(dimension_semantics=semantics, vmem_limit_bytes=VMEM_LIMIT)


def _silu(x):
    return x * (1.0 / (1.0 + jnp.exp(-x)))


def _softplus(x):
    return jnp.maximum(x, 0.0) + jnp.log1p(jnp.exp(-jnp.abs(x)))


def _tri(n, lower):
    r = lax.broadcasted_iota(jnp.int32, (n, n), 0)
    c = lax.broadcasted_iota(jnp.int32, (n, n), 1)
    return ((r >= c) if lower else (r <= c)).astype(F32)


def _ada_body(c_ref, w_ref, b_ref, o_ref):
    a = _silu(c_ref[...]).astype(BF16)
    o_ref[...] = jnp.dot(a, w_ref[...].astype(BF16), preferred_element_type=F32) + b_ref[...]


def _ada_mod(c_all, ada_w, ada_b, tn=512):
    depth, d, n = ada_w.shape
    rows = c_all.shape[0]
    return pl.pallas_call(
        _ada_body,
        grid=(depth, n // tn),
        in_specs=[
            pl.BlockSpec((rows, d), lambda l, j: (0, 0)),
            pl.BlockSpec((None, d, tn), lambda l, j: (l, 0, j)),
            pl.BlockSpec((None, 1, tn), lambda l, j: (l, 0, j)),
        ],
        out_specs=pl.BlockSpec((None, rows, tn), lambda l, j: (l, 0, j)),
        out_shape=jax.ShapeDtypeStruct((depth, rows, n), F32),
        compiler_params=_params(("arbitrary", "arbitrary")),
        name="ada_mod",
    )(c_all, ada_w, ada_b.reshape(depth, 1, n))


def _norm_mod_body(x_ref, g_ref, sc_ref, sh_ref, o_ref):
    x = x_ref[...]
    y = x * lax.rsqrt(jnp.mean(x * x, axis=-1, keepdims=True) + EPS) * g_ref[...]
    o_ref[...] = (y * (1.0 + sc_ref[...]) + sh_ref[...]).astype(o_ref.dtype)


def _norm_body(x_ref, g_ref, o_ref):
    x = x_ref[...]
    o_ref[...] = (x * lax.rsqrt(jnp.mean(x * x, axis=-1, keepdims=True) + EPS) * g_ref[...]).astype(o_ref.dtype)


def _row_or_batch_spec(arr, tr, tn, rows_per_batch):
    if arr.ndim == 3:
        tiles_per_batch = rows_per_batch // tr
        return pl.BlockSpec((None, 1, tn), lambda i, j: (i // tiles_per_batch, 0, j))
    return pl.BlockSpec((tr, tn), lambda i, j: (i, j))


def _norm_mod(x, g, scale, shift, rows_per_batch, tr):
    m, d = x.shape
    return pl.pallas_call(
        _norm_mod_body,
        grid=(m // tr, 1),
        in_specs=[
            pl.BlockSpec((tr, d), lambda i, j: (i, 0)),
            pl.BlockSpec((1, d), lambda i, j: (0, 0)),
            _row_or_batch_spec(scale, tr, d, rows_per_batch),
            _row_or_batch_spec(shift, tr, d, rows_per_batch),
        ],
        out_specs=pl.BlockSpec((tr, d), lambda i, j: (i, 0)),
        out_shape=jax.ShapeDtypeStruct((m, d), BF16),
        compiler_params=_params(("arbitrary", "arbitrary")),
        name="norm_mod",
    )(x, g.reshape(1, d), scale, shift)


def _final_norm(x, g, tr):
    m, d = x.shape
    return pl.pallas_call(
        _norm_body,
        grid=(m // tr,),
        in_specs=[pl.BlockSpec((tr, d), lambda i: (i, 0)), pl.BlockSpec((1, d), lambda i: (0, 0))],
        out_specs=pl.BlockSpec((tr, d), lambda i: (i, 0)),
        out_shape=jax.ShapeDtypeStruct((m, d), F32),
        compiler_params=_params(("arbitrary",)),
        name="final_norm",
    )(x, g.reshape(1, d))


def _mm_plain_body(a_ref, w_ref, *o_refs):
    acc = jnp.dot(a_ref[...], w_ref[...], preferred_element_type=F32)
    for o_ref in o_refs:
        o_ref[...] = acc.astype(o_ref.dtype)


def _mm_logsig_body(a_ref, w_ref, b_ref, o_ref):
    acc = jnp.dot(a_ref[...], w_ref[...], preferred_element_type=F32) + b_ref[...]
    o_ref[...] = -_softplus(-acc)


def _mm_resid_body(a_ref, w_ref, x_ref, g_ref, o_ref):
    acc = jnp.dot(a_ref[...], w_ref[...], preferred_element_type=F32)
    o_ref[...] = x_ref[...] + g_ref[...] * acc


def _mm_specs(a, w, col0, tm, tn):
    k = a.shape[1]
    off = col0 // tn
    return [pl.BlockSpec((tm, k), lambda i, j: (i, 0)),
            pl.BlockSpec((k, tn), lambda i, j: (0, j + off))]


def _mm_plain(a, w, col0, n, out_dtypes, tm, tn, name):
    m = a.shape[0]
    return pl.pallas_call(
        _mm_plain_body,
        grid=(m // tm, n // tn),
        in_specs=_mm_specs(a, w, col0, tm, tn),
        out_specs=[pl.BlockSpec((tm, tn), lambda i, j: (i, j)) for _ in out_dtypes],
        out_shape=[jax.ShapeDtypeStruct((m, n), dt) for dt in out_dtypes],
        compiler_params=_params(("arbitrary", "arbitrary")),
        name=name,
    )(a, w)


def _mm_logsig(a, w, bias, tm, name):
    m = a.shape[0]
    n = w.shape[1]
    return pl.pallas_call(
        _mm_logsig_body,
        grid=(m // tm, 1),
        in_specs=_mm_specs(a, w, 0, tm, n) + [pl.BlockSpec((1, n), lambda i, j: (0, 0))],
        out_specs=pl.BlockSpec((tm, n), lambda i, j: (i, 0)),
        out_shape=jax.ShapeDtypeStruct((m, n), F32),
        compiler_params=_params(("arbitrary", "arbitrary")),
        name=name,
    )(a, w, bias)


def _mm_resid(a, w, x, gate, rows_per_batch, tm, tn, name):
    m, n = x.shape
    return pl.pallas_call(
        _mm_resid_body,
        grid=(m // tm, n // tn),
        in_specs=_mm_specs(a, w, 0, tm, tn) + [
            pl.BlockSpec((tm, tn), lambda i, j: (i, j)),
            _row_or_batch_spec(gate, tm, tn, rows_per_batch),
        ],
        out_specs=pl.BlockSpec((tm, tn), lambda i, j: (i, j)),
        out_shape=jax.ShapeDtypeStruct((m, n), F32),
        compiler_params=_params(("arbitrary", "arbitrary")),
        name=name,
    )(a, w, x, gate)


def _cumsum_body(x_ref, f_ref, carry_ref):
    @pl.when(pl.program_id(1) == 0)
    def _():
        carry_ref[...] = jnp.zeros_like(carry_ref)

    f = jnp.dot(_tri(PAGE, True), x_ref[...], precision=lax.Precision.HIGHEST,
                preferred_element_type=F32) + carry_ref[0:1, :]
    f_ref[...] = f
    carry_ref[...] = jnp.broadcast_to(f[PAGE - 1:PAGE, :], carry_ref.shape)


def _time_cumsum(x):
    b, t, h = x.shape
    spec = pl.BlockSpec((None, PAGE, h), lambda i, p: (i, p, 0))
    return pl.pallas_call(
        _cumsum_body,
        grid=(b, t // PAGE),
        in_specs=[spec],
        out_specs=spec,
        out_shape=jax.ShapeDtypeStruct(x.shape, F32),
        scratch_shapes=[pltpu.VMEM((SUBLANES, h), F32)],
        compiler_params=_params(("arbitrary", "arbitrary")),
        name="logf_cumsum",
    )(x)


def _fox_prompt_body(q_ref, k_ref, v_ref, z_ref, f_ref, o_ref, m_ref, l_ref, acc_ref, fq_ref, *, tq, tk):
    t = q_ref.shape[0]
    scale = FOX_HEAD_DIM ** -0.5
    nrep = tk // LANES
    for qi in range(t // tq):
        q0 = qi * tq
        q = q_ref[q0:q0 + tq, :]
        fq_ref[...] = jnp.broadcast_to(f_ref[qi], (LANES, tq)).T
        m_ref[...] = jnp.full_like(m_ref, -jnp.inf)
        l_ref[...] = jnp.zeros_like(l_ref)
        acc_ref[...] = jnp.zeros_like(acc_ref)

        def kv_step(j, k0, masked):
            kb = k_ref[pl.ds(k0, tk), :]
            vb = v_ref[pl.ds(k0, tk), :]
            s = lax.dot_general(q, kb, (((1,), (1,)), ((), ())), preferred_element_type=F32) * scale
            s = s + (pltpu.repeat(fq_ref[...], nrep, 1) - f_ref[j])
            if masked:
                r = lax.broadcasted_iota(jnp.int32, (tq, tk), 0)
                c = lax.broadcasted_iota(jnp.int32, (tq, tk), 1)
                s = jnp.where(c <= r, s, -jnp.inf)
            m_prev = m_ref[...]
            m_next = jnp.maximum(m_prev, jnp.max(s, axis=1, keepdims=True))
            p = jnp.exp(s - pltpu.repeat(m_next, nrep, 1))
            alpha = jnp.exp(m_prev - m_next)
            l_ref[...] = alpha * l_ref[...] + jnp.sum(p, axis=1, keepdims=True)
            acc_ref[...] = alpha * acc_ref[...] + jnp.dot(p.astype(BF16), vb, preferred_element_type=F32)
            m_ref[...] = m_next

        def off_diag(j, carry):
            kv_step(j, pl.multiple_of(j * tk, tk), False)
            return carry

        if qi > 0:
            lax.fori_loop(0, qi, off_diag, 0)
        kv_step(qi, q0, True)
        o = acc_ref[...] / l_ref[...]
        zb = z_ref[q0:q0 + tq, :].astype(F32)
        o_ref[q0:q0 + tq, :] = (o * _silu(zb)).astype(o_ref.dtype)


def _fox_prompt_attn(q, k, v, z, f_t, batch, t, tq=512, tk=512):
    assert tq == tk and t % tq == 0
    hd = FOX_HEAD_DIM
    nk = t // tk
    f_t = f_t.reshape(batch, FOX_HEADS, nk, 1, tk)
    tb = t
    body = functools.partial(_fox_prompt_body, tq=tq, tk=tk)
    row_spec = pl.BlockSpec((tb, hd), lambda b, h: (b, h))
    return pl.pallas_call(
        body,
        grid=(batch, FOX_HEADS),
        in_specs=[row_spec, row_spec, row_spec, row_spec,
                  pl.BlockSpec((None, None, nk, 1, tk), lambda b, h: (b, h, 0, 0, 0))],
        out_specs=row_spec,
        out_shape=jax.ShapeDtypeStruct(q.shape, BF16),
        scratch_shapes=[pltpu.VMEM((tq, LANES), F32), pltpu.VMEM((tq, LANES), F32),
                        pltpu.VMEM((tq, hd), F32), pltpu.VMEM((tq, LANES), F32)],
        compiler_params=_params(("arbitrary", "arbitrary")),
        name="fox_prompt_attn",
    )(q, k, v, z, f_t)


QPAD = 16


DECODE_HEAD_GROUP = 8


def _fox_decode_body(tbl_ref, q_ref, k_ref, v_ref, lf_ref, ln_ref, lnt_ref, kn_ref, vn_ref, z_ref, o_ref,
                     m_ref, l_ref, acc_ref, fq_ref, carry_ref, *, n_new):
    del tbl_ref
    p = pl.program_id(1)
    scale = FOX_HEAD_DIM ** -0.5
    hd = FOX_HEAD_DIM
    grows = DECODE_HEAD_GROUP * QPAD

    def update(load_k, load_v, key_bias, masked):
        for g in range(FOX_HEADS // DECODE_HEAD_GROUP):
            heads = range(g * DECODE_HEAD_GROUP, (g + 1) * DECODE_HEAD_GROUP)
            r0 = g * grows
            s_parts = []
            for h in heads:
                qh = q_ref[:, h * hd:(h + 1) * hd]
                s_h = lax.dot_general(qh, load_k(h), (((1,), (1,)), ((), ())), preferred_element_type=F32)
                s_parts.append(s_h * scale + jnp.broadcast_to(key_bias[h:h + 1, :], (QPAD, PAGE)))
            s = jnp.concatenate(s_parts, axis=0) + fq_ref[r0:r0 + grows, :]
            if masked:
                r = lax.broadcasted_iota(jnp.int32, (grows, PAGE), 0) & (QPAD - 1)
                c = lax.broadcasted_iota(jnp.int32, (grows, PAGE), 1)
                s = jnp.where((c <= r) & (c < n_new), s, -jnp.inf)
            m_prev = m_ref[r0:r0 + grows, :]
            m_next = jnp.maximum(m_prev, jnp.max(s, axis=1, keepdims=True))
            pr = jnp.exp(s - m_next)
            alpha = jnp.exp(m_prev - m_next)
            l_ref[r0:r0 + grows, :] = alpha * l_ref[r0:r0 + grows, :] + jnp.sum(pr, axis=1, keepdims=True)
            pb = pr.astype(BF16)
            pv = [jnp.dot(pb[i * QPAD:(i + 1) * QPAD, :], load_v(h), preferred_element_type=F32)
                  for i, h in enumerate(heads)]
            acc_ref[r0:r0 + grows, :] = alpha * acc_ref[r0:r0 + grows, :] + jnp.concatenate(pv, axis=0)
            m_ref[r0:r0 + grows, :] = m_next

    @pl.when(p == 0)
    def _():
        m_ref[...] = jnp.full_like(m_ref, -jnp.inf)
        l_ref[...] = jnp.zeros_like(l_ref)
        acc_ref[...] = jnp.zeros_like(acc_ref)
        carry_ref[...] = jnp.zeros_like(carry_ref)
        ln = ln_ref[...]
        row = lax.broadcasted_iota(jnp.int32, ln.shape, 0)
        cn = jnp.zeros_like(ln)
        for i in range(n_new):
            cn = cn + jnp.where(row >= i, ln[i:i + 1, :], 0.0)
        for h in range(FOX_HEADS):
            fq_ref[h * QPAD:(h + 1) * QPAD, :] = jnp.broadcast_to(cn[:, h:h + 1], (QPAD, LANES))
        cnt = jnp.dot(lnt_ref[...], _tri(PAGE, False), precision=lax.Precision.HIGHEST,
                      preferred_element_type=F32)
        update(lambda h: kn_ref[:, h * hd:(h + 1) * hd],
               lambda h: vn_ref[:, h * hd:(h + 1) * hd],
               -cnt, True)

    lft = lf_ref[...]
    suffix = jnp.dot(lft, 1.0 - _tri(PAGE, False), precision=lax.Precision.HIGHEST,
                     preferred_element_type=F32) + carry_ref[...]
    carry_ref[...] = carry_ref[...] + jnp.sum(lft, axis=1, keepdims=True)
    update(lambda h: k_ref[pl.ds(h, PAGE, stride=FOX_HEADS), :].astype(BF16),
           lambda h: v_ref[pl.ds(h, PAGE, stride=FOX_HEADS), :].astype(BF16),
           suffix, False)

    @pl.when(p == pl.num_programs(1) - 1)
    def _():
        o = acc_ref[...] / l_ref[...]
        for h in range(FOX_HEADS):
            zb = z_ref[:, h * hd:(h + 1) * hd].astype(F32)
            o_ref[:, h * hd:(h + 1) * hd] = (o[h * QPAD:(h + 1) * QPAD, :] * _silu(zb)).astype(o_ref.dtype)


def _fox_decode_attn(table, layer, q16, cache_k, cache_v, logf_t, ln16, ln_t, k_new, v_new, z16, n_new):
    b, n_pages = table.shape
    width = FOX_HEADS * FOX_HEAD_DIM
    rows = FOX_HEADS * QPAD
    body = functools.partial(_fox_decode_body, n_new=n_new)
    per_b = lambda shape: pl.BlockSpec((None,) + shape, lambda i, p, tbl: (i, 0, 0))
    page = pl.BlockSpec((None, None, PAGE * FOX_HEADS, FOX_HEAD_DIM),
                        lambda i, p, tbl: (layer, tbl[i, n_pages - 1 - p], 0, 0))
    return pl.pallas_call(
        body,
        grid_spec=pltpu.PrefetchScalarGridSpec(
            num_scalar_prefetch=1,
            grid=(b, n_pages),
            in_specs=[
                per_b((QPAD, width)), page, page,
                pl.BlockSpec((None, FOX_HEADS, PAGE), lambda i, p, tbl: (tbl[i, n_pages - 1 - p], 0, 0)),
                per_b((QPAD, FOX_HEADS)), per_b((FOX_HEADS, PAGE)),
                per_b((PAGE, width)), per_b((PAGE, width)),
                per_b((QPAD, width)),
            ],
            out_specs=per_b((QPAD, width)),
            scratch_shapes=[pltpu.VMEM((rows, LANES), F32), pltpu.VMEM((rows, LANES), F32),
                            pltpu.VMEM((rows, FOX_HEAD_DIM), F32), pltpu.VMEM((rows, LANES), F32),
                            pltpu.VMEM((FOX_HEADS, LANES), F32)],
        ),
        out_shape=jax.ShapeDtypeStruct((b, QPAD, width), BF16),
        compiler_params=_params(("arbitrary", "arbitrary")),
        name="fox_decode_attn",
    )(table, q16, cache_k, cache_v, logf_t, ln16, ln_t, k_new, v_new, z16)


def _conv_body(x_ref, init_ref, w_ref, b_ref, o_ref, carry_ref):
    @pl.when(pl.program_id(2) == 0)
    def _():
        carry_ref[...] = init_ref[...]

    x = x_ref[...].astype(F32)
    tt = x.shape[0]
    c8 = carry_ref[...]
    row = lax.broadcasted_iota(jnp.int32, c8.shape, 0)
    acc = b_ref[...] + x * w_ref[SSD_CONV - 1:SSD_CONV, :]
    for shift in range(1, SSD_CONV):
        rolled = pltpu.roll(x, shift, 0)
        top = jnp.where(row < shift, pltpu.roll(c8, shift, 0), rolled[0:SUBLANES, :])
        shifted = jnp.concatenate([top, rolled[SUBLANES:, :]], axis=0)
        acc = acc + shifted * w_ref[SSD_CONV - 1 - shift:SSD_CONV - shift, :]
    o_ref[...] = _silu(acc).astype(o_ref.dtype)
    carry_ref[...] = x[tt - SUBLANES:, :]


def _conv_silu(x, init, w, bias, batch, t, tt, tc=2048):
    c = x.shape[1]
    nt = t // tt
    return pl.pallas_call(
        _conv_body,
        grid=(batch, c // tc, nt),
        in_specs=[
            pl.BlockSpec((tt, tc), lambda b, j, i: (b * nt + i, j)),
            pl.BlockSpec((None, SUBLANES, tc), lambda b, j, i: (b, 0, j)),
            pl.BlockSpec((SSD_CONV, tc), lambda b, j, i: (0, j)),
            pl.BlockSpec((1, tc), lambda b, j, i: (0, j)),
        ],
        out_specs=pl.BlockSpec((tt, tc), lambda b, j, i: (b * nt + i, j)),
        out_shape=jax.ShapeDtypeStruct(x.shape, BF16),
        scratch_shapes=[pltpu.VMEM((SUBLANES, tc), F32)],
        compiler_params=_params(("arbitrary", "arbitrary", "arbitrary")),
        name="ssd_conv_silu",
    )(x, init, w, bias.reshape(1, c))


def _ssd_body(*refs, valid, has_init):
    if has_init:
        (x_ref, b_ref, c_ref, dt_ref, dtt_ref, z_ref, bias_ref, biast_ref, alog_ref, alogt_ref,
         d_ref, nw_ref, s0_ref, y_ref, sout_ref, st_ref, yacc_ref) = refs
    else:
        (x_ref, b_ref, c_ref, dt_ref, dtt_ref, z_ref, bias_ref, biast_ref, alog_ref, alogt_ref,
         d_ref, nw_ref, y_ref, sout_ref, st_ref, yacc_ref) = refs
        s0_ref = None
    ci = pl.program_id(2)
    L = SSD_CHUNK
    npairs = SSD_HEADS_PER_GROUP // 2

    @pl.when(ci == 0)
    def _():
        if has_init:
            for k in range(npairs):
                st_ref[:, k * LANES:(k + 1) * LANES] = s0_ref[k * LANES:(k + 1) * LANES, :].T
        else:
            st_ref[...] = jnp.zeros_like(st_ref)

    dt = _softplus(dt_ref[...] + bias_ref[...])
    dtt = _softplus(dtt_ref[...] + biast_ref[...])
    if valid < L:
        dt = jnp.where(lax.broadcasted_iota(jnp.int32, dt.shape, 0) < valid, dt, 0.0)
        dtt = jnp.where(lax.broadcasted_iota(jnp.int32, dtt.shape, 1) < valid, dtt, 0.0)
    acum = jnp.dot(_tri(L, True), dt * (-jnp.exp(alog_ref[...])),
                   precision=lax.Precision.HIGHEST, preferred_element_type=F32)
    acumt = jnp.dot(dtt * (-jnp.exp(alogt_ref[...])), _tri(L, False),
                    precision=lax.Precision.HIGHEST, preferred_element_type=F32)
    last_t = acumt[:, L - 1:L]
    wt = jnp.exp(last_t - acumt) * dtt
    chunk_decay = jnp.broadcast_to(jnp.exp(last_t), (SSD_HEADS_PER_GROUP, LANES))

    bm = b_ref[...]
    cm = c_ref[...]
    cb = lax.dot_general(cm, bm, (((1,), (1,)), ((), ())), preferred_element_type=F32)
    cm32 = cm.astype(F32)
    bmt = bm.astype(F32).T
    causal = lax.broadcasted_iota(jnp.int32, (L, L), 0) >= lax.broadcasted_iota(jnp.int32, (L, L), 1)
    lane = lax.broadcasted_iota(jnp.int32, (L, LANES), 1)
    first_head = lane < SSD_HEAD_DIM
    first_head_row = lax.broadcasted_iota(jnp.int32, (1, LANES), 1) < SSD_HEAD_DIM

    for k in range(npairs):
        xp = x_ref[:, k * LANES:(k + 1) * LANES]
        sp = st_ref[:, k * LANES:(k + 1) * LANES]
        rhs = jnp.concatenate([xp, sp.astype(BF16)], axis=0)
        ys, ups = [], []
        for r in (2 * k, 2 * k + 1):
            a_col = jnp.broadcast_to(acum[:, r:r + 1], (L, L))
            a_row = acumt[r:r + 1, :]
            dec = jnp.where(causal, jnp.exp(a_col - a_row), 0.0)
            m = cb * dec * dtt[r:r + 1, :]
            ec = cm32 * jnp.exp(a_col)
            lhs = jnp.concatenate([m, ec], axis=1).astype(BF16)
            ys.append(jnp.dot(lhs, rhs, preferred_element_type=F32))
            bw = (bmt * wt[r:r + 1, :]).astype(BF16)
            ups.append(jnp.dot(bw, xp, preferred_element_type=F32))
        yacc_ref[:, k * LANES:(k + 1) * LANES] = jnp.where(first_head, ys[0], ys[1])
        dec_row = jnp.where(first_head_row, chunk_decay[2 * k:2 * k + 1, :],
                            chunk_decay[2 * k + 1:2 * k + 2, :])
        st_ref[:, k * LANES:(k + 1) * LANES] = sp * dec_row + jnp.where(first_head, ups[0], ups[1])

    y = yacc_ref[...] + d_ref[...] * x_ref[...].astype(F32)
    y = y * _silu(z_ref[...].astype(F32))
    y = y * lax.rsqrt(jnp.mean(y * y, axis=-1, keepdims=True) + EPS) * nw_ref[...]
    y_ref[...] = y.astype(y_ref.dtype)

    @pl.when(ci == pl.num_programs(2) - 1)
    def _():
        for k in range(npairs):
            sout_ref[k * LANES:(k + 1) * LANES, :] = st_ref[:, k * LANES:(k + 1) * LANES].T


def _ssd_scan(act, dt_g, dtt_g, z, dt_bias, a_log, d_skip, norm_w, state0, batch, t, valid):
    L = SSD_CHUNK
    nc = t // L
    gw = SSD_GROUP_WIDTH
    hpg = SSD_HEADS_PER_GROUP
    g = SSD_GROUPS
    inner = g * gw
    b_blk0 = inner // SSD_STATE
    c_blk0 = b_blk0 + g
    has_init = state0 is not None
    body = functools.partial(_ssd_body, valid=valid, has_init=has_init)
    grp = lambda shape: pl.BlockSpec((None,) + shape, lambda b, gi, c: (gi, 0, 0))
    in_specs = [
        pl.BlockSpec((L, gw), lambda b, gi, c: (b * nc + c, gi)),
        pl.BlockSpec((L, SSD_STATE), lambda b, gi, c: (b * nc + c, b_blk0 + gi)),
        pl.BlockSpec((L, SSD_STATE), lambda b, gi, c: (b * nc + c, c_blk0 + gi)),
        pl.BlockSpec((None, L, hpg), lambda b, gi, c: (gi, b * nc + c, 0)),
        pl.BlockSpec((None, hpg, L), lambda b, gi, c: (gi, 0, b * nc + c)),
        pl.BlockSpec((L, gw), lambda b, gi, c: (b * nc + c, gi)),
        grp((1, hpg)), grp((hpg, 1)), grp((1, hpg)), grp((hpg, 1)),
        grp((1, gw)), grp((1, gw)),
    ]
    args = [act, act, act, dt_g, dtt_g, z,
            dt_bias.reshape(g, 1, hpg), dt_bias.reshape(g, hpg, 1),
            a_log.reshape(g, 1, hpg), a_log.reshape(g, hpg, 1),
            jnp.repeat(d_skip, SSD_HEAD_DIM).reshape(g, 1, gw), norm_w.reshape(g, 1, gw)]
    state_spec = pl.BlockSpec((None, gw, SSD_STATE), lambda b, gi, c: (b, gi, 0))
    if has_init:
        in_specs.append(state_spec)
        args.append(state0.reshape(batch, g * gw, SSD_STATE))
    y, s_out = pl.pallas_call(
        body,
        grid=(batch, g, nc),
        in_specs=in_specs,
        out_specs=[pl.BlockSpec((L, gw), lambda b, gi, c: (b * nc + c, gi)), state_spec],
        out_shape=[jax.ShapeDtypeStruct((batch * t, inner), BF16),
                   jax.ShapeDtypeStruct((batch, g * gw, SSD_STATE), F32)],
        scratch_shapes=[pltpu.VMEM((SSD_STATE, gw), F32), pltpu.VMEM((L, gw), F32)],
        compiler_params=_params(("arbitrary", "arbitrary", "arbitrary")),
        name="ssd_scan",
    )(*args)
    return y, s_out.reshape(batch, g * hpg, SSD_HEAD_DIM, SSD_STATE)


def _mod_parts(mod_l, nb_prompt, nb_sample, rows_per_sample, d):
    def prompt(a):
        return a[:nb_prompt].reshape(nb_prompt, 1, d)

    def sample(a):
        return jnp.repeat(a[nb_prompt:nb_prompt + nb_sample], rows_per_sample, axis=0)

    parts = [mod_l[:, i * d:(i + 1) * d] for i in range(3)]
    return [prompt(a) for a in parts], [sample(a) for a in parts]


def _pad_rows(a, batch, t, t_pad):
    c = a.shape[-1]
    a = a.reshape(batch, t, c)
    return jnp.pad(a, ((0, 0), (0, t_pad - t), (0, 0))).reshape(batch * t_pad, c)


def kernel(x_prompt, x_sample, c_prompt, c_sample, cache_k, cache_v, cache_logf, state_ssm, state_conv,
           page_table, norm_w, ada_w, ada_b, fox_w_in, fox_b_f, fox_w_out, ssd_w_in, ssd_conv_w, ssd_conv_b,
           ssd_dt_bias, ssd_a_log, ssd_d, ssd_norm_w, ssd_w_out, final_norm_w):
    bp, t, d = x_prompt.shape
    bs, ts, _ = x_sample.shape
    depth = norm_w.shape[0]
    mp, ms = bp * t, bs * ts
    width = FOX_HEADS * FOX_HEAD_DIM
    inner = SSD_GROUPS * SSD_GROUP_WIDTH
    conv_dim = inner + 2 * SSD_GROUPS * SSD_STATE
    n_pages = page_table.shape[1]
    tm = 512

    xp = x_prompt.reshape(mp, d)
    xs = x_sample.reshape(ms, d)

    c_all = jnp.concatenate([c_prompt, c_sample], axis=0)
    c_all = jnp.pad(c_all, ((0, (-c_all.shape[0]) % SUBLANES), (0, 0)))
    mod = _ada_mod(c_all, ada_w, ada_b)

    cache_k_rows = cache_k.reshape(cache_k.shape[0], cache_k.shape[1], PAGE * FOX_HEADS, FOX_HEAD_DIM)
    cache_v_rows = cache_v.reshape(cache_v.shape[0], cache_v.shape[1], PAGE * FOX_HEADS, FOX_HEAD_DIM)

    outs = {name: [] for name in ("kp", "vp", "fp", "ks", "vs", "fs", "sp", "cp", "ss", "cs")}
    for i in range(depth):
        j = i // 2
        (sh_p, sc_p, g_p), (sh_s, sc_s, g_s) = _mod_parts(mod[i], bp, bs, ts, d)
        hp = _norm_mod(xp, norm_w[i], sc_p, sh_p, t, 256)
        hs = _norm_mod(xs, norm_w[i], sc_s, sh_s, ts, ms)
        if i % 2 == 0:
            w_in = fox_w_in[j].astype(BF16)
            w_f = jnp.pad(fox_w_in[j][:, 4 * width:], ((0, 0), (0, LANES - FOX_HEADS))).astype(BF16)
            b_f = jnp.pad(fox_b_f[j], (0, LANES - FOX_HEADS)).reshape(1, LANES)
            w_out = fox_w_out[j].astype(BF16)

            def project(h, tmm):
                (q,) = _mm_plain(h, w_in, 0, width, [BF16], tmm, 512, "fox_q")
                k32, k16 = _mm_plain(h, w_in, width, width, [F32, BF16], tmm, 512, "fox_k")
                v32, v16 = _mm_plain(h, w_in, 2 * width, width, [F32, BF16], tmm, 512, "fox_v")
                (z,) = _mm_plain(h, w_in, 3 * width, width, [BF16], tmm, 512, "fox_z")
                logf = _mm_logsig(h, w_f, b_f, tmm, "fox_logf")[:, :FOX_HEADS]
                return q, k32, k16, v32, v16, z, logf

            q, k32, k16, v32, v16, z, logf = project(hp, tm)
            f_cum = _time_cumsum(logf.reshape(bp, t, FOX_HEADS))
            og = _fox_prompt_attn(q, k16, v16, z, jnp.swapaxes(f_cum, 1, 2), bp, t, tq=min(512, t), tk=min(512, t))
            xp = _mm_resid(og, w_out, xp, g_p, t, tm, 512, "fox_out")
            outs["kp"].append(k32.reshape(bp, t, FOX_HEADS, FOX_HEAD_DIM))
            outs["vp"].append(v32.reshape(bp, t, FOX_HEADS, FOX_HEAD_DIM))
            outs["fp"].append(logf.reshape(bp, t, FOX_HEADS))

            q, k32, k16, v32, v16, z, logf = project(hs, ms)
            def pad_q(a, rows):
                return jnp.pad(a.reshape(bs, ts, -1), ((0, 0), (0, rows - ts), (0, 0)))

            ln16 = pad_q(logf, QPAD)
            ln_t = jnp.swapaxes(pad_q(logf, PAGE), 1, 2)
            og = _fox_decode_attn(page_table, j, pad_q(q, QPAD), cache_k_rows, cache_v_rows,
                                  jnp.swapaxes(cache_logf[j], 1, 2), ln16, ln_t,
                                  pad_q(k16, PAGE), pad_q(v16, PAGE), pad_q(z, QPAD), ts)
            og = og[:, :ts].reshape(ms, width)
            xs = _mm_resid(og, w_out, xs, g_s, ts, ms, 512, "fox_out_s")
            outs["ks"].append(k32.reshape(bs, ts, FOX_HEADS, FOX_HEAD_DIM))
            outs["vs"].append(v32.reshape(bs, ts, FOX_HEADS, FOX_HEAD_DIM))
            outs["fs"].append(logf.reshape(bs, ts, FOX_HEADS))
        else:
            w_in = ssd_w_in[j].astype(BF16)
            w_out = ssd_w_out[j].astype(BF16)
            n_heads = SSD_GROUPS * SSD_HEADS_PER_GROUP

            def mixer(h, x_res, gate, batch, tt_, tmm, conv_prev, state0, suffix):
                m = batch * tt_
                (zz,) = _mm_plain(h, w_in, 0, inner, [BF16], tmm, 512, "ssd_z" + suffix)
                (xbc,) = _mm_plain(h, w_in, inner, conv_dim, [F32], tmm, 512, "ssd_xbc" + suffix)
                (dt_raw,) = _mm_plain(h, w_in, inner + conv_dim, n_heads, [F32], tmm, n_heads, "ssd_dt" + suffix)
                t_pad = -(-tt_ // SSD_CHUNK) * SSD_CHUNK
                if t_pad != tt_:
                    zz_p, xbc_p, dt_p = (_pad_rows(a, batch, tt_, t_pad) for a in (zz, xbc, dt_raw))
                else:
                    zz_p, xbc_p, dt_p = zz, xbc, dt_raw
                if conv_prev is None:
                    init = jnp.zeros((batch, SUBLANES, conv_dim), F32)
                    new_conv = xbc.reshape(batch, tt_, conv_dim)[:, tt_ - (SSD_CONV - 1):]
                else:
                    init = jnp.pad(conv_prev, ((0, 0), (SUBLANES - (SSD_CONV - 1), 0), (0, 0)))
                    full = jnp.concatenate([conv_prev, xbc.reshape(batch, tt_, conv_dim)], axis=1)
                    new_conv = full[:, tt_:]
                act = _conv_silu(xbc_p, init, ssd_conv_w[j], ssd_conv_b[j], batch, t_pad, min(512, t_pad))
                dt_g = dt_p.reshape(batch * t_pad, SSD_GROUPS, SSD_HEADS_PER_GROUP).transpose(1, 0, 2)
                dtt_g = dt_g.transpose(0, 2, 1)
                y, new_state = _ssd_scan(act, dt_g, dtt_g, zz_p, ssd_dt_bias[j], ssd_a_log[j], ssd_d[j],
                                         ssd_norm_w[j], state0, batch, t_pad, min(tt_, SSD_CHUNK))
                if t_pad != tt_:
                    y = y.reshape(batch, t_pad, inner)[:, :tt_].reshape(m, inner)
                x_new = _mm_resid(y, w_out, x_res, gate, tt_, tmm, 512, "ssd_out" + suffix)
                return x_new, new_conv, new_state

            xp, cvp, smp = mixer(hp, xp, g_p, bp, t, tm, None, None, "")
            xs, cvs, sms = mixer(hs, xs, g_s, bs, ts, ms, state_conv[j], state_ssm[j], "_s")
            outs["sp"].append(smp)
            outs["cp"].append(cvp)
            outs["ss"].append(sms)
            outs["cs"].append(cvs)

    y_prompt = _final_norm(xp, final_norm_w, 256).reshape(bp, t, d)
    y_sample = _final_norm(xs, final_norm_w, ms).reshape(bs, ts, d)
    return (y_prompt, y_sample,
            jnp.stack(outs["kp"]), jnp.stack(outs["vp"]), jnp.stack(outs["fp"]),
            jnp.stack(outs["ks"]), jnp.stack(outs["vs"]), jnp.stack(outs["fs"]),
            jnp.stack(outs["sp"]), jnp.stack(outs["cp"]), jnp.stack(outs["ss"]), jnp.stack(outs["cs"]))
```

```python
import functools
import math

import jax
import jax.numpy as jnp
from jax import lax
from jax.experimental import pallas as pl
from jax.experimental.pallas import tpu as pltpu

F32 = jnp.float32
BF16 = jnp.bfloat16

EPS = 1e-6
LOG2E = math.log2(math.e)
LANES = 128
SUBLANES = 8
PAGE = 128
FOX_HEADS = 32
FOX_HEAD_DIM = 128
SSD_GROUPS = 8
SSD_HEAD_DIM = 64
SSD_STATE = 128
SSD_HEADS_PER_GROUP = 16
SSD_GROUP_WIDTH = SSD_HEADS_PER_GROUP * SSD_HEAD_DIM
SSD_CHUNK = 128
SSD_CONV = 4
V7X_VMEM_BYTES = 64 * 1024 * 1024
VMEM_LIMIT = V7X_VMEM_BYTES * 7 // 8
MM_TILE_BUDGET = 48 * 1024 * 1024


def _params(semantics):
    return pltpu.CompilerParams(dimension_semantics=semantics, vmem_limit_bytes=VMEM_LIMIT)


def _silu(x):
    return x * (1.0 / (1.0 + jnp.exp(-x)))


def _softplus(x):
    return jnp.maximum(x, 0.0) + jnp.log1p(jnp.exp(-jnp.abs(x)))


def _tri(n, lower):
    r = lax.broadcasted_iota(jnp.int32, (n, n), 0)
    c = lax.broadcasted_iota(jnp.int32, (n, n), 1)
    return ((r >= c) if lower else (r <= c)).astype(F32)


def _lane_tile(x, reps):
    return x if reps == 1 else jnp.concatenate([x] * reps, axis=1)


def _ada_body(c_ref, w_ref, b_ref, o_ref):
    a = _silu(c_ref[...]).astype(BF16)
    o_ref[...] = jnp.dot(a, w_ref[...].astype(BF16), preferred_element_type=F32) + b_ref[...]


def _ada_mod(c_all, ada_w, ada_b, tn=512):
    depth, d, n = ada_w.shape
    rows = c_all.shape[0]
    return pl.pallas_call(
        _ada_body,
        grid=(depth, n // tn),
        in_specs=[
            pl.BlockSpec((rows, d), lambda l, j: (0, 0)),
            pl.BlockSpec((None, d, tn), lambda l, j: (l, 0, j)),
            pl.BlockSpec((None, 1, tn), lambda l, j: (l, 0, j)),
        ],
        out_specs=pl.BlockSpec((None, rows, tn), lambda l, j: (l, 0, j)),
        out_shape=jax.ShapeDtypeStruct((depth, rows, n), F32),
        compiler_params=_params(("arbitrary", "arbitrary")),
        name="ada_mod",
    )(c_all, ada_w, ada_b.reshape(depth, 1, n))


def _norm_mod_body(x_ref, g_ref, sc_ref, sh_ref, o_ref):
    x = x_ref[...]
    y = x * lax.rsqrt(jnp.mean(x * x, axis=-1, keepdims=True) + EPS) * g_ref[...]
    o_ref[...] = (y * (1.0 + sc_ref[...]) + sh_ref[...]).astype(o_ref.dtype)


def _norm_body(x_ref, g_ref, o_ref):
    x = x_ref[...]
    o_ref[...] = (x * lax.rsqrt(jnp.mean(x * x, axis=-1, keepdims=True) + EPS) * g_ref[...]).astype(o_ref.dtype)


def _row_or_batch_spec(arr, tr, tn, rows_per_batch):
    if arr.ndim == 3:
        tiles_per_batch = rows_per_batch // tr
        return pl.BlockSpec((None, 1, tn), lambda i, j: (i // tiles_per_batch, 0, j))
    return pl.BlockSpec((tr, tn), lambda i, j: (i, j))


def _norm_mod(x, g, scale, shift, rows_per_batch, tr):
    m, d = x.shape
    return pl.pallas_call(
        _norm_mod_body,
        grid=(m // tr, 1),
        in_specs=[
            pl.BlockSpec((tr, d), lambda i, j: (i, 0)),
            pl.BlockSpec((1, d), lambda i, j: (0, 0)),
            _row_or_batch_spec(scale, tr, d, rows_per_batch),
            _row_or_batch_spec(shift, tr, d, rows_per_batch),
        ],
        out_specs=pl.BlockSpec((tr, d), lambda i, j: (i, 0)),
        out_shape=jax.ShapeDtypeStruct((m, d), BF16),
        compiler_params=_params(("arbitrary", "arbitrary")),
        name="norm_mod",
    )(x, g.reshape(1, d), scale, shift)


def _final_norm(x, g, tr):
    m, d = x.shape
    return pl.pallas_call(
        _norm_body,
        grid=(m // tr,),
        in_specs=[pl.BlockSpec((tr, d), lambda i: (i, 0)), pl.BlockSpec((1, d), lambda i: (0, 0))],
        out_specs=pl.BlockSpec((tr, d), lambda i: (i, 0)),
        out_shape=jax.ShapeDtypeStruct((m, d), F32),
        compiler_params=_params(("arbitrary",)),
        name="final_norm",
    )(x, g.reshape(1, d))


def _mm_tiles(m, k, n, out_bytes_per_elem):
    for tm, tn in ((1024, 1024), (1024, 512), (512, 512), (512, 256), (256, 256)):
        tm_, tn_ = min(tm, m), min(tn, n)
        if m % tm_ or n % tn_:
            continue
        need = 2 * (tm_ * k * 2 + k * tn_ * 2 + tm_ * tn_ * out_bytes_per_elem)
        if need <= MM_TILE_BUDGET:
            return tm_, tn_
    raise ValueError(f"no matmul tiling for {(m, k, n)}")


def _mm_plain_body(a_ref, w_ref, *o_refs):
    acc = jnp.dot(a_ref[...], w_ref[...], preferred_element_type=F32)
    for o_ref in o_refs:
        o_ref[...] = acc.astype(o_ref.dtype)


def _mm_tail_body(a_ref, w_ref, o_ref, tail_ref):
    acc = jnp.dot(a_ref[...], w_ref[...], preferred_element_type=F32)
    o_ref[...] = acc.astype(o_ref.dtype)
    tail_ref[...] = acc[acc.shape[0] - SUBLANES:, :]


def _mm_kv_body(a_ref, w_ref, *refs):
    o32_ref, o16_ref = refs[-2:]
    acc = jnp.dot(a_ref[...], w_ref[...], preferred_element_type=F32)
    o32_ref[...] = acc
    o16_ref[...] = acc.astype(o16_ref.dtype)


def _mm_logsig_body(a_ref, w_ref, b_ref, o_ref):
    acc = jnp.dot(a_ref[...], w_ref[...], preferred_element_type=F32) + b_ref[...]
    o_ref[...] = -_softplus(-acc)


def _mm_resid_body(a_ref, w_ref, x_ref, g_ref, o_ref):
    acc = jnp.dot(a_ref[...], w_ref[...], preferred_element_type=F32)
    o_ref[...] = x_ref[...] + g_ref[...] * acc


def _mm_specs(a, layer, col0, tm, tn):
    k = a.shape[1]
    off = col0 // tn
    return [pl.BlockSpec((tm, k), lambda i, j: (i, 0)),
            pl.BlockSpec((None, k, tn), lambda i, j: (layer, 0, j + off))]


def _mm_plain(a, w, layer, col0, n, out_dtypes, name):
    m, k = a.shape
    tm, tn = _mm_tiles(m, k, n, sum(jnp.dtype(dt).itemsize for dt in out_dtypes))
    return pl.pallas_call(
        _mm_plain_body,
        grid=(m // tm, n // tn),
        in_specs=_mm_specs(a, layer, col0, tm, tn),
        out_specs=[pl.BlockSpec((tm, tn), lambda i, j: (i, j)) for _ in out_dtypes],
        out_shape=[jax.ShapeDtypeStruct((m, n), dt) for dt in out_dtypes],
        compiler_params=_params(("arbitrary", "arbitrary")),
        name=name,
    )(a, w)


def _mm_tail(a, w, layer, col0, n, rows_per_batch, name):
    m, k = a.shape
    tm, tn = _mm_tiles(rows_per_batch, k, n, 2)
    tiles_per_batch = rows_per_batch // tm
    return pl.pallas_call(
        _mm_tail_body,
        grid=(m // tm, n // tn),
        in_specs=_mm_specs(a, layer, col0, tm, tn),
        out_specs=[pl.BlockSpec((tm, tn), lambda i, j: (i, j)),
                   pl.BlockSpec((None, SUBLANES, tn), lambda i, j: (i // tiles_per_batch, 0, j))],
        out_shape=[jax.ShapeDtypeStruct((m, n), BF16),
                   jax.ShapeDtypeStruct((m // rows_per_batch, SUBLANES, n), F32)],
        compiler_params=_params(("arbitrary", "arbitrary")),
        name=name,
    )(a, w)


def _mm_kv(a, w, layer, n_layers, col0, n, stacked, name):
    m, k = a.shape
    tm, tn = _mm_tiles(m, k, n, 6)
    in_specs = _mm_specs(a, layer, col0, tm, tn)
    args = [a, w]
    aliases = {}
    if stacked is not None:
        in_specs.append(pl.BlockSpec(memory_space=pl.ANY))
        args.append(stacked)
        aliases = {2: 0}
    return pl.pallas_call(
        _mm_kv_body,
        grid=(m // tm, n // tn),
        in_specs=in_specs,
        out_specs=[pl.BlockSpec((None, tm, tn), lambda i, j: (layer, i, j)),
                   pl.BlockSpec((tm, tn), lambda i, j: (i, j))],
        out_shape=[jax.ShapeDtypeStruct((n_layers, m, n), F32), jax.ShapeDtypeStruct((m, n), BF16)],
        input_output_aliases=aliases,
        compiler_params=_params(("arbitrary", "arbitrary")),
        name=name,
    )(*args)


def _mm_logsig(a, w, bias, name):
    m, k = a.shape
    n = w.shape[-1]
    tm = min(m, 1024)
    return pl.pallas_call(
        _mm_logsig_body,
        grid=(m // tm, 1),
        in_specs=_mm_specs(a, 0, 0, tm, n) + [pl.BlockSpec((1, n), lambda i, j: (0, 0))],
        out_specs=pl.BlockSpec((tm, n), lambda i, j: (i, 0)),
        out_shape=jax.ShapeDtypeStruct((m, n), F32),
        compiler_params=_params(("arbitrary", "arbitrary")),
        name=name,
    )(a, w, bias)


def _mm_resid(a, w, layer, x, gate, rows_per_batch, name):
    m, n = x.shape
    k = a.shape[1]
    tm, tn = _mm_tiles(min(m, rows_per_batch) if gate.ndim == 3 else m, k, n, 8)
    return pl.pallas_call(
        _mm_resid_body,
        grid=(m // tm, n // tn),
        in_specs=_mm_specs(a, layer, 0, tm, tn) + [
            pl.BlockSpec((tm, tn), lambda i, j: (i, j)),
            _row_or_batch_spec(gate, tm, tn, rows_per_batch),
        ],
        out_specs=pl.BlockSpec((tm, tn), lambda i, j: (i, j)),
        out_shape=jax.ShapeDtypeStruct((m, n), F32),
        compiler_params=_params(("arbitrary", "arbitrary")),
        name=name,
    )(a, w, x, gate)


def _cumsum_body(x_ref, f_ref, carry_ref):
    @pl.when(pl.program_id(1) == 0)
    def _():
        carry_ref[...] = jnp.zeros_like(carry_ref)

    f = jnp.dot(_tri(PAGE, True), x_ref[...], precision=lax.Precision.HIGHEST,
                preferred_element_type=F32) + carry_ref[0:1, :]
    f_ref[...] = f
    carry_ref[...] = jnp.broadcast_to(f[PAGE - 1:PAGE, :], carry_ref.shape)


def _time_cumsum(x):
    b, t, h = x.shape
    spec = pl.BlockSpec((None, PAGE, h), lambda i, p: (i, p, 0))
    return pl.pallas_call(
        _cumsum_body,
        grid=(b, t // PAGE),
        in_specs=[spec],
        out_specs=spec,
        out_shape=jax.ShapeDtypeStruct(x.shape, F32),
        scratch_shapes=[pltpu.VMEM((SUBLANES, h), F32)],
        compiler_params=_params(("arbitrary", "arbitrary")),
        name="logf_cumsum",
    )(x)


def _fox_prompt_body(q_ref, k_ref, v_ref, z_ref, f_ref, o_ref, m_ref, l_ref, acc_ref, fq_ref, *, tq, tk):
    t = q_ref.shape[0]
    qk_scale = FOX_HEAD_DIM ** -0.5 * LOG2E
    nrep = tk // LANES
    for qi in range(t // tq):
        q0 = qi * tq
        q = q_ref[q0:q0 + tq, :]
        fq_ref[...] = jnp.broadcast_to(f_ref[qi] * LOG2E, (LANES, tq)).T
        m_ref[...] = jnp.full_like(m_ref, -jnp.inf)
        l_ref[...] = jnp.zeros_like(l_ref)
        acc_ref[...] = jnp.zeros_like(acc_ref)

        def kv_step(j, k0, masked):
            kb = k_ref[pl.ds(k0, tk), :]
            vb = v_ref[pl.ds(k0, tk), :]
            s = lax.dot_general(q, kb, (((1,), (1,)), ((), ())), preferred_element_type=F32) * qk_scale
            s = s - f_ref[j] * LOG2E
            if masked:
                r = lax.broadcasted_iota(jnp.int32, (tq, tk), 0)
                c = lax.broadcasted_iota(jnp.int32, (tq, tk), 1)
                s = jnp.where(c <= r, s, -jnp.inf)
            fq = fq_ref[...]
            m_prev = m_ref[...]
            m_next = jnp.maximum(m_prev, jnp.max(s, axis=1, keepdims=True) + fq)
            p = jnp.exp2(s + _lane_tile(fq - m_next, nrep))
            alpha = jnp.exp2(m_prev - m_next)
            l_ref[...] = alpha * l_ref[...] + jnp.sum(p, axis=1, keepdims=True)
            acc_ref[...] = alpha * acc_ref[...] + jnp.dot(p.astype(BF16), vb, preferred_element_type=F32)
            m_ref[...] = m_next

        for j in range(qi):
            kv_step(j, j * tk, False)
        kv_step(qi, q0, True)
        o = acc_ref[...] / l_ref[...]
        zb = z_ref[q0:q0 + tq, :].astype(F32)
        o_ref[q0:q0 + tq, :] = (o * _silu(zb)).astype(o_ref.dtype)


def _fox_prompt_attn(q, k, v, z, f_t, batch, t, tq=512, tk=512):
    assert tq == tk and t % tq == 0
    hd = FOX_HEAD_DIM
    nk = t // tk
    f_t = f_t.reshape(batch, FOX_HEADS, nk, 1, tk)
    body = functools.partial(_fox_prompt_body, tq=tq, tk=tk)
    row_spec = pl.BlockSpec((t, hd), lambda b, h: (b, h))
    return pl.pallas_call(
        body,
        grid=(batch, FOX_HEADS),
        in_specs=[row_spec, row_spec, row_spec, row_spec,
                  pl.BlockSpec((None, None, nk, 1, tk), lambda b, h: (b, h, 0, 0, 0))],
        out_specs=row_spec,
        out_shape=jax.ShapeDtypeStruct(q.shape, BF16),
        scratch_shapes=[pltpu.VMEM((tq, LANES), F32), pltpu.VMEM((tq, LANES), F32),
                        pltpu.VMEM((tq, hd), F32), pltpu.VMEM((tq, LANES), F32)],
        compiler_params=_params(("arbitrary", "arbitrary")),
        name="fox_prompt_attn",
    )(q, k, v, z, f_t)


QPAD = 16
DECODE_HEAD_GROUP = 8


def _fox_decode_body(tbl_ref, q_ref, k_ref, v_ref, lf_ref, ln_ref, lnt_ref, kn_ref, vn_ref, z_ref, o_ref,
                     m_ref, l_ref, acc_ref, fq_ref, carry_ref, *, n_new):
    del tbl_ref
    p = pl.program_id(1)
    scale = FOX_HEAD_DIM ** -0.5
    hd = FOX_HEAD_DIM
    grows = DECODE_HEAD_GROUP * QPAD

    def update(load_k, load_v, key_bias, masked):
        for g in range(FOX_HEADS // DECODE_HEAD_GROUP):
            heads = range(g * DECODE_HEAD_GROUP, (g + 1) * DECODE_HEAD_GROUP)
            r0 = g * grows
            s_parts = []
            for h in heads:
                qh = q_ref[:, h * hd:(h + 1) * hd]
                s_h = lax.dot_general(qh, load_k(h), (((1,), (1,)), ((), ())), preferred_element_type=F32)
                s_parts.append(s_h * scale + jnp.broadcast_to(key_bias[h:h + 1, :], (QPAD, PAGE)))
            s = jnp.concatenate(s_parts, axis=0) + fq_ref[r0:r0 + grows, :]
            if masked:
                r = lax.broadcasted_iota(jnp.int32, (grows, PAGE), 0) & (QPAD - 1)
                c = lax.broadcasted_iota(jnp.int32, (grows, PAGE), 1)
                s = jnp.where((c <= r) & (c < n_new), s, -jnp.inf)
            m_prev = m_ref[r0:r0 + grows, :]
            m_next = jnp.maximum(m_prev, jnp.max(s, axis=1, keepdims=True))
            pr = jnp.exp(s - m_next)
            alpha = jnp.exp(m_prev - m_next)
            l_ref[r0:r0 + grows, :] = alpha * l_ref[r0:r0 + grows, :] + jnp.sum(pr, axis=1, keepdims=True)
            pb = pr.astype(BF16)
            pv = [jnp.dot(pb[i * QPAD:(i + 1) * QPAD, :], load_v(h), preferred_element_type=F32)
                  for i, h in enumerate(heads)]
            acc_ref[r0:r0 + grows, :] = alpha * acc_ref[r0:r0 + grows, :] + jnp.concatenate(pv, axis=0)
            m_ref[r0:r0 + grows, :] = m_next

    @pl.when(p == 0)
    def _():
        m_ref[...] = jnp.full_like(m_ref, -jnp.inf)
        l_ref[...] = jnp.zeros_like(l_ref)
        acc_ref[...] = jnp.zeros_like(acc_ref)
        carry_ref[...] = jnp.zeros_like(carry_ref)
        ln = ln_ref[...]
        row = lax.broadcasted_iota(jnp.int32, ln.shape, 0)
        cn = jnp.zeros_like(ln)
        for i in range(n_new):
            cn = cn + jnp.where(row >= i, ln[i:i + 1, :], 0.0)
        for h in range(FOX_HEADS):
            fq_ref[h * QPAD:(h + 1) * QPAD, :] = jnp.broadcast_to(cn[:, h:h + 1], (QPAD, LANES))
        cnt = jnp.dot(lnt_ref[...], _tri(PAGE, False), precision=lax.Precision.HIGHEST,
                      preferred_element_type=F32)
        update(lambda h: kn_ref[:, h * hd:(h + 1) * hd],
               lambda h: vn_ref[:, h * hd:(h + 1) * hd],
               -cnt, True)

    lft = lf_ref[...]
    suffix = jnp.dot(lft, 1.0 - _tri(PAGE, False), precision=lax.Precision.HIGHEST,
                     preferred_element_type=F32) + carry_ref[...]
    carry_ref[...] = carry_ref[...] + jnp.sum(lft, axis=1, keepdims=True)
    update(lambda h: k_ref[pl.ds(h, PAGE, stride=FOX_HEADS), :].astype(BF16),
           lambda h: v_ref[pl.ds(h, PAGE, stride=FOX_HEADS), :].astype(BF16),
           suffix, False)

    @pl.when(p == pl.num_programs(1) - 1)
    def _():
        o = acc_ref[...] / l_ref[...]
        for h in range(FOX_HEADS):
            zb = z_ref[:, h * hd:(h + 1) * hd].astype(F32)
            o_ref[:, h * hd:(h + 1) * hd] = (o[h * QPAD:(h + 1) * QPAD, :] * _silu(zb)).astype(o_ref.dtype)


def _fox_decode_attn(table, layer, q16, cache_k, cache_v, logf_t, ln16, ln_t, k_new, v_new, z16, n_new):
    b, n_pages = table.shape
    width = FOX_HEADS * FOX_HEAD_DIM
    rows = FOX_HEADS * QPAD
    body = functools.partial(_fox_decode_body, n_new=n_new)
    per_b = lambda shape: pl.BlockSpec((None,) + shape, lambda i, p, tbl: (i, 0, 0))
    page = pl.BlockSpec((None, None, PAGE * FOX_HEADS, FOX_HEAD_DIM),
                        lambda i, p, tbl: (layer, tbl[i, n_pages - 1 - p], 0, 0))
    return pl.pallas_call(
        body,
        grid_spec=pltpu.PrefetchScalarGridSpec(
            num_scalar_prefetch=1,
            grid=(b, n_pages),
            in_specs=[
                per_b((QPAD, width)), page, page,
                pl.BlockSpec((None, FOX_HEADS, PAGE), lambda i, p, tbl: (tbl[i, n_pages - 1 - p], 0, 0)),
                per_b((QPAD, FOX_HEADS)), per_b((FOX_HEADS, PAGE)),
                per_b((PAGE, width)), per_b((PAGE, width)),
                per_b((QPAD, width)),
            ],
            out_specs=per_b((QPAD, width)),
            scratch_shapes=[pltpu.VMEM((rows, LANES), F32), pltpu.VMEM((rows, LANES), F32),
                            pltpu.VMEM((rows, FOX_HEAD_DIM), F32), pltpu.VMEM((rows, LANES), F32),
                            pltpu.VMEM((FOX_HEADS, LANES), F32)],
        ),
        out_shape=jax.ShapeDtypeStruct((b, QPAD, width), BF16),
        compiler_params=_params(("arbitrary", "arbitrary")),
        name="fox_decode_attn",
    )(table, q16, cache_k, cache_v, logf_t, ln16, ln_t, k_new, v_new, z16)


def _conv_body(x_ref, init_ref, w_ref, b_ref, o_ref, carry_ref):
    @pl.when(pl.program_id(2) == 0)
    def _():
        carry_ref[...] = init_ref[...]

    x = x_ref[...].astype(F32)
    tt = x.shape[0]
    c8 = carry_ref[...]
    row = lax.broadcasted_iota(jnp.int32, c8.shape, 0)
    acc = b_ref[...] + x * w_ref[SSD_CONV - 1:SSD_CONV, :]
    for shift in range(1, SSD_CONV):
        rolled = pltpu.roll(x, shift, 0)
        top = jnp.where(row < shift, pltpu.roll(c8, shift, 0), rolled[0:SUBLANES, :])
        shifted = jnp.concatenate([top, rolled[SUBLANES:, :]], axis=0)
        acc = acc + shifted * w_ref[SSD_CONV - 1 - shift:SSD_CONV - shift, :]
    o_ref[...] = _silu(acc).astype(o_ref.dtype)
    carry_ref[...] = x[tt - SUBLANES:, :]


def _conv_silu(x, init, w, bias, batch, t, tt, tc=2048):
    c = x.shape[1]
    nt = t // tt
    return pl.pallas_call(
        _conv_body,
        grid=(batch, c // tc, nt),
        in_specs=[
            pl.BlockSpec((tt, tc), lambda b, j, i: (b * nt + i, j)),
            pl.BlockSpec((None, SUBLANES, tc), lambda b, j, i: (b, 0, j)),
            pl.BlockSpec((SSD_CONV, tc), lambda b, j, i: (0, j)),
            pl.BlockSpec((1, tc), lambda b, j, i: (0, j)),
        ],
        out_specs=pl.BlockSpec((tt, tc), lambda b, j, i: (b * nt + i, j)),
        out_shape=jax.ShapeDtypeStruct(x.shape, BF16),
        scratch_shapes=[pltpu.VMEM((SUBLANES, tc), F32)],
        compiler_params=_params(("arbitrary", "arbitrary", "arbitrary")),
        name="ssd_conv_silu",
    )(x, init, w, bias.reshape(1, c))


def _ssd_body(*refs, valid, has_init):
    if has_init:
        (x_ref, b_ref, c_ref, dt_ref, dtt_ref, z_ref, bias_ref, biast_ref, alog_ref, alogt_ref,
         d_ref, nw_ref, s0_ref, y_ref, sout_ref, st_ref, yacc_ref) = refs
    else:
        (x_ref, b_ref, c_ref, dt_ref, dtt_ref, z_ref, bias_ref, biast_ref, alog_ref, alogt_ref,
         d_ref, nw_ref, y_ref, sout_ref, st_ref, yacc_ref) = refs
        s0_ref = None
    ci = pl.program_id(2)
    L = SSD_CHUNK
    npairs = SSD_HEADS_PER_GROUP // 2

    @pl.when(ci == 0)
    def _():
        if has_init:
            for k in range(npairs):
                st_ref[:, k * LANES:(k + 1) * LANES] = s0_ref[k * LANES:(k + 1) * LANES, :].T
        else:
            st_ref[...] = jnp.zeros_like(st_ref)

    dt = _softplus(dt_ref[...] + bias_ref[...])
    dtt = _softplus(dtt_ref[...] + biast_ref[...])
    if valid < L:
        dt = jnp.where(lax.broadcasted_iota(jnp.int32, dt.shape, 0) < valid, dt, 0.0)
        dtt = jnp.where(lax.broadcasted_iota(jnp.int32, dtt.shape, 1) < valid, dtt, 0.0)
    acum = jnp.dot(_tri(L, True), dt * (-jnp.exp(alog_ref[...])),
                   precision=lax.Precision.HIGHEST, preferred_element_type=F32) * LOG2E
    acumt = jnp.dot(dtt * (-jnp.exp(alogt_ref[...])), _tri(L, False),
                    precision=lax.Precision.HIGHEST, preferred_element_type=F32) * LOG2E
    last_t = acumt[:, L - 1:L]
    wt = jnp.exp2(last_t - acumt) * dtt
    chunk_decay = jnp.broadcast_to(jnp.exp2(last_t), (SSD_HEADS_PER_GROUP, LANES))
    src_term = acumt - jnp.log2(dtt)

    bm = b_ref[...]
    cm = c_ref[...]
    cb = lax.dot_general(cm, bm, (((1,), (1,)), ((), ())), preferred_element_type=F32)
    cm32 = cm.astype(F32)
    bmt = bm.astype(F32).T
    causal = lax.broadcasted_iota(jnp.int32, (L, L), 0) >= lax.broadcasted_iota(jnp.int32, (L, L), 1)
    lane = lax.broadcasted_iota(jnp.int32, (L, LANES), 1)
    first_head = lane < SSD_HEAD_DIM
    first_head_row = lax.broadcasted_iota(jnp.int32, (1, LANES), 1) < SSD_HEAD_DIM

    for k in range(npairs):
        xp = x_ref[:, k * LANES:(k + 1) * LANES]
        sp = st_ref[:, k * LANES:(k + 1) * LANES]
        rhs = jnp.concatenate([xp, sp.astype(BF16)], axis=0)
        ys, ups = [], []
        for r in (2 * k, 2 * k + 1):
            a_col = jnp.broadcast_to(acum[:, r:r + 1], (L, L))
            m = cb * jnp.exp2(jnp.where(causal, a_col - src_term[r:r + 1, :], -jnp.inf))
            ec = cm32 * jnp.exp2(a_col)
            lhs = jnp.concatenate([m, ec], axis=1).astype(BF16)
            ys.append(jnp.dot(lhs, rhs, preferred_element_type=F32))
            bw = (bmt * wt[r:r + 1, :]).astype(BF16)
            ups.append(jnp.dot(bw, xp, preferred_element_type=F32))
        yacc_ref[:, k * LANES:(k + 1) * LANES] = jnp.where(first_head, ys[0], ys[1])
        dec_row = jnp.where(first_head_row, chunk_decay[2 * k:2 * k + 1, :],
                            chunk_decay[2 * k + 1:2 * k + 2, :])
        st_ref[:, k * LANES:(k + 1) * LANES] = sp * dec_row + jnp.where(first_head, ups[0], ups[1])

    y = yacc_ref[...] + d_ref[...] * x_ref[...].astype(F32)
    y = y * _silu(z_ref[...].astype(F32))
    y = y * lax.rsqrt(jnp.mean(y * y, axis=-1, keepdims=True) + EPS) * nw_ref[...]
    y_ref[...] = y.astype(y_ref.dtype)

    @pl.when(ci == pl.num_programs(2) - 1)
    def _():
        for k in range(npairs):
            sout_ref[k * LANES:(k + 1) * LANES, :] = st_ref[:, k * LANES:(k + 1) * LANES].T


def _ssd_scan(act, dt_g, dtt_g, z, dt_bias, a_log, d_skip, norm_w, state0, batch, t, valid):
    L = SSD_CHUNK
    nc = t // L
    gw = SSD_GROUP_WIDTH
    hpg = SSD_HEADS_PER_GROUP
    g = SSD_GROUPS
    inner = g * gw
    b_blk0 = inner // SSD_STATE
    c_blk0 = b_blk0 + g
    has_init = state0 is not None
    body = functools.partial(_ssd_body, valid=valid, has_init=has_init)
    grp = lambda shape: pl.BlockSpec((None,) + shape, lambda b, gi, c: (gi, 0, 0))
    in_specs = [
        pl.BlockSpec((L, gw), lambda b, gi, c: (b * nc + c, gi)),
        pl.BlockSpec((L, SSD_STATE), lambda b, gi, c: (b * nc + c, b_blk0 + gi)),
        pl.BlockSpec((L, SSD_STATE), lambda b, gi, c: (b * nc + c, c_blk0 + gi)),
        pl.BlockSpec((None, L, hpg), lambda b, gi, c: (gi, b * nc + c, 0)),
        pl.BlockSpec((None, hpg, L), lambda b, gi, c: (gi, 0, b * nc + c)),
        pl.BlockSpec((L, gw), lambda b, gi, c: (b * nc + c, gi)),
        grp((1, hpg)), grp((hpg, 1)), grp((1, hpg)), grp((hpg, 1)),
        grp((1, gw)), grp((1, gw)),
    ]
    args = [act, act, act, dt_g, dtt_g, z,
            dt_bias.reshape(g, 1, hpg), dt_bias.reshape(g, hpg, 1),
            a_log.reshape(g, 1, hpg), a_log.reshape(g, hpg, 1),
            jnp.repeat(d_skip, SSD_HEAD_DIM).reshape(g, 1, gw), norm_w.reshape(g, 1, gw)]
    state_spec = pl.BlockSpec((None, gw, SSD_STATE), lambda b, gi, c: (b, gi, 0))
    if has_init:
        in_specs.append(state_spec)
        args.append(state0.reshape(batch, g * gw, SSD_STATE))
    y, s_out = pl.pallas_call(
        body,
        grid=(batch, g, nc),
        in_specs=in_specs,
        out_specs=[pl.BlockSpec((L, gw), lambda b, gi, c: (b * nc + c, gi)), state_spec],
        out_shape=[jax.ShapeDtypeStruct((batch * t, inner), BF16),
                   jax.ShapeDtypeStruct((batch, g * gw, SSD_STATE), F32)],
        scratch_shapes=[pltpu.VMEM((SSD_STATE, gw), F32), pltpu.VMEM((L, gw), F32)],
        compiler_params=_params(("arbitrary", "arbitrary", "arbitrary")),
        name="ssd_scan",
    )(*args)
    return y, s_out.reshape(batch, g * hpg, SSD_HEAD_DIM, SSD_STATE)


def _mod_parts(mod_l, nb_prompt, nb_sample, rows_per_sample, d):
    def prompt(a):
        return a[:nb_prompt].reshape(nb_prompt, 1, d)

    def sample(a):
        return jnp.repeat(a[nb_prompt:nb_prompt + nb_sample], rows_per_sample, axis=0)

    parts = [mod_l[:, i * d:(i + 1) * d] for i in range(3)]
    return [prompt(a) for a in parts], [sample(a) for a in parts]


def _pad_rows(a, batch, t, t_pad):
    c = a.shape[-1]
    a = a.reshape(batch, t, c)
    return jnp.pad(a, ((0, 0), (0, t_pad - t), (0, 0))).reshape(batch * t_pad, c)


def kernel(x_prompt, x_sample, c_prompt, c_sample, cache_k, cache_v, cache_logf, state_ssm, state_conv,
           page_table, norm_w, ada_w, ada_b, fox_w_in, fox_b_f, fox_w_out, ssd_w_in, ssd_conv_w, ssd_conv_b,
           ssd_dt_bias, ssd_a_log, ssd_d, ssd_norm_w, ssd_w_out, final_norm_w):
    bp, t, d = x_prompt.shape
    bs, ts, _ = x_sample.shape
    depth = norm_w.shape[0]
    n_fox = fox_w_in.shape[0]
    mp, ms = bp * t, bs * ts
    width = FOX_HEADS * FOX_HEAD_DIM
    inner = SSD_GROUPS * SSD_GROUP_WIDTH
    conv_dim = inner + 2 * SSD_GROUPS * SSD_STATE
    n_heads = SSD_GROUPS * SSD_HEADS_PER_GROUP

    xp = x_prompt.reshape(mp, d)
    xs = x_sample.reshape(ms, d)

    c_all = jnp.concatenate([c_prompt, c_sample], axis=0)
    c_all = jnp.pad(c_all, ((0, (-c_all.shape[0]) % SUBLANES), (0, 0)))
    mod = _ada_mod(c_all, ada_w, ada_b)

    fox_w_in16 = fox_w_in.astype(BF16)
    fox_w_out16 = fox_w_out.astype(BF16)
    ssd_w_in16 = ssd_w_in.astype(BF16)
    ssd_w_out16 = ssd_w_out.astype(BF16)
    fox_w_f16 = jnp.pad(fox_w_in[:, :, 4 * width:], ((0, 0), (0, 0), (0, LANES - FOX_HEADS))).astype(BF16)

    cache_k_rows = cache_k.reshape(cache_k.shape[0], cache_k.shape[1], PAGE * FOX_HEADS, FOX_HEAD_DIM)
    cache_v_rows = cache_v.reshape(cache_v.shape[0], cache_v.shape[1], PAGE * FOX_HEADS, FOX_HEAD_DIM)

    kp_all = vp_all = None
    outs = {name: [] for name in ("fp", "ks", "vs", "fs", "sp", "cp", "ss", "cs")}
    for i in range(depth):
        j = i // 2
        (sh_p, sc_p, g_p), (sh_s, sc_s, g_s) = _mod_parts(mod[i], bp, bs, ts, d)
        hp = _norm_mod(xp, norm_w[i], sc_p, sh_p, t, 256)
        hs = _norm_mod(xs, norm_w[i], sc_s, sh_s, ts, ms)
        if i % 2 == 0:
            b_f = jnp.pad(fox_b_f[j], (0, LANES - FOX_HEADS)).reshape(1, LANES)

            def logf_of(h):
                return _mm_logsig(h, fox_w_f16[j:j + 1], b_f, "fox_logf")[:, :FOX_HEADS]

            (q,) = _mm_plain(hp, fox_w_in16, j, 0, width, [BF16], "fox_q")
            kp_all, k16 = _mm_kv(hp, fox_w_in16, j, n_fox, width, width, kp_all, "fox_k")
            vp_all, v16 = _mm_kv(hp, fox_w_in16, j, n_fox, 2 * width, width, vp_all, "fox_v")
            (z,) = _mm_plain(hp, fox_w_in16, j, 3 * width, width, [BF16], "fox_z")
            logf = logf_of(hp)
            f_cum = _time_cumsum(logf.reshape(bp, t, FOX_HEADS))
            og = _fox_prompt_attn(q, k16, v16, z, jnp.swapaxes(f_cum, 1, 2), bp, t, tq=min(512, t), tk=min(512, t))
            xp = _mm_resid(og, fox_w_out16, j, xp, g_p, t, "fox_out")
            outs["fp"].append(logf.reshape(bp, t, FOX_HEADS))

            (q,) = _mm_plain(hs, fox_w_in16, j, 0, width, [BF16], "fox_q_s")
            k32, k16 = _mm_plain(hs, fox_w_in16, j, width, width, [F32, BF16], "fox_k_s")
            v32, v16 = _mm_plain(hs, fox_w_in16, j, 2 * width, width, [F32, BF16], "fox_v_s")
            (z,) = _mm_plain(hs, fox_w_in16, j, 3 * width, width, [BF16], "fox_z_s")
            logf = logf_of(hs)

            def pad_q(a, rows):
                return jnp.pad(a.reshape(bs, ts, -1), ((0, 0), (0, rows - ts), (0, 0)))

            ln16 = pad_q(logf, QPAD)
            ln_t = jnp.swapaxes(pad_q(logf, PAGE), 1, 2)
            og = _fox_decode_attn(page_table, j, pad_q(q, QPAD), cache_k_rows, cache_v_rows,
                                  jnp.swapaxes(cache_logf[j], 1, 2), ln16, ln_t,
                                  pad_q(k16, PAGE), pad_q(v16, PAGE), pad_q(z, QPAD), ts)
            og = og[:, :ts].reshape(ms, width)
            xs = _mm_resid(og, fox_w_out16, j, xs, g_s, ts, "fox_out_s")
            outs["ks"].append(k32.reshape(bs, ts, FOX_HEADS, FOX_HEAD_DIM))
            outs["vs"].append(v32.reshape(bs, ts, FOX_HEADS, FOX_HEAD_DIM))
            outs["fs"].append(logf.reshape(bs, ts, FOX_HEADS))
        else:
            def mixer(h, x_res, gate, batch, tt_, conv_prev, state0, suffix):
                m = batch * tt_
                (zz,) = _mm_plain(h, ssd_w_in16, j, 0, inner, [BF16], "ssd_z" + suffix)
                (dt_raw,) = _mm_plain(h, ssd_w_in16, j, inner + conv_dim, n_heads, [F32], "ssd_dt" + suffix)
                t_pad = -(-tt_ // SSD_CHUNK) * SSD_CHUNK
                if conv_prev is None:
                    xbc, tail = _mm_tail(h, ssd_w_in16, j, inner, conv_dim, tt_, "ssd_xbc" + suffix)
                    init = jnp.zeros((batch, SUBLANES, conv_dim), F32)
                    new_conv = tail[:, SUBLANES - (SSD_CONV - 1):]
                else:
                    (xbc,) = _mm_plain(h, ssd_w_in16, j, inner, conv_dim, [F32], "ssd_xbc" + suffix)
                    init = jnp.pad(conv_prev, ((0, 0), (SUBLANES - (SSD_CONV - 1), 0), (0, 0)))
                    full = jnp.concatenate([conv_prev, xbc.reshape(batch, tt_, conv_dim)], axis=1)
                    new_conv = full[:, tt_:]
                if t_pad != tt_:
                    zz, xbc, dt_raw = (_pad_rows(a, batch, tt_, t_pad) for a in (zz, xbc, dt_raw))
                act = _conv_silu(xbc, init, ssd_conv_w[j], ssd_conv_b[j], batch, t_pad, min(512, t_pad))
                dt_g = dt_raw.reshape(batch * t_pad, SSD_GROUPS, SSD_HEADS_PER_GROUP).transpose(1, 0, 2)
                dtt_g = dt_g.transpose(0, 2, 1)
                y, new_state = _ssd_scan(act, dt_g, dtt_g, zz, ssd_dt_bias[j], ssd_a_log[j], ssd_d[j],
                                         ssd_norm_w[j], state0, batch, t_pad, min(tt_, SSD_CHUNK))
                if t_pad != tt_:
                    y = y.reshape(batch, t_pad, inner)[:, :tt_].reshape(m, inner)
                x_new = _mm_resid(y, ssd_w_out16, j, x_res, gate, tt_, "ssd_out" + suffix)
                return x_new, new_conv, new_state

            xp, cvp, smp = mixer(hp, xp, g_p, bp, t, None, None, "")
            xs, cvs, sms = mixer(hs, xs, g_s, bs, ts, state_conv[j], state_ssm[j], "_s")
            outs["sp"].append(smp)
            outs["cp"].append(cvp)
            outs["ss"].append(sms)
            outs["cs"].append(cvs)

    y_prompt = _final_norm(xp, final_norm_w, 256).reshape(bp, t, d)
    y_sample = _final_norm(xs, final_norm_w, ms).reshape(bs, ts, d)
    kv_shape = (n_fox, bp, t, FOX_HEADS, FOX_HEAD_DIM)
    return (y_prompt, y_sample,
            kp_all.reshape(kv_shape), vp_all.reshape(kv_shape), jnp.stack(outs["fp"]),
            jnp.stack(outs["ks"]), jnp.stack(outs["vs"]), jnp.stack(outs["fs"]),
            jnp.stack(outs["sp"]), jnp.stack(outs["cp"]), jnp.stack(outs["ss"]), jnp.stack(outs["cs"]))
```

```python
import functools
import math

import jax
import jax.numpy as jnp
from jax import lax
from jax.experimental import pallas as pl
from jax.experimental.pallas import tpu as pltpu

F32 = jnp.float32
BF16 = jnp.bfloat16

EPS = 1e-6
LOG2E = math.log2(math.e)
LANES = 128
SUBLANES = 8
PAGE = 128
FOX_HEADS = 32
FOX_HEAD_DIM = 128
SSD_GROUPS = 8
SSD_HEAD_DIM = 64
SSD_STATE = 128
SSD_HEADS_PER_GROUP = 16
SSD_GROUP_WIDTH = SSD_HEADS_PER_GROUP * SSD_HEAD_DIM
SSD_CHUNK = 128
SSD_CONV = 4
V7X_VMEM_BYTES = 64 * 1024 * 1024
VMEM_LIMIT = V7X_VMEM_BYTES * 7 // 8
MM_TILE_BUDGET = 48 * 1024 * 1024


def _params(semantics):
    return pltpu.CompilerParams(dimension_semantics=semantics, vmem_limit_bytes=VMEM_LIMIT)


def _silu(x):
    return x * (1.0 / (1.0 + jnp.exp2(x * -LOG2E)))


def _softplus(x):
    return jnp.maximum(x, 0.0) + jnp.log1p(jnp.exp(-jnp.abs(x)))


def _tri(n, lower):
    r = lax.broadcasted_iota(jnp.int32, (n, n), 0)
    c = lax.broadcasted_iota(jnp.int32, (n, n), 1)
    return ((r >= c) if lower else (r <= c)).astype(F32)


def _lane_tile(x, reps):
    return x if reps == 1 else jnp.concatenate([x] * reps, axis=1)


def _ada_body(c_ref, w_ref, b_ref, o_ref):
    a = _silu(c_ref[...]).astype(BF16)
    o_ref[...] = jnp.dot(a, w_ref[...].astype(BF16), preferred_element_type=F32) + b_ref[...]


def _ada_mod(c_all, ada_w, ada_b, tn=512):
    depth, d, n = ada_w.shape
    rows = c_all.shape[0]
    return pl.pallas_call(
        _ada_body,
        grid=(depth, n // tn),
        in_specs=[
            pl.BlockSpec((rows, d), lambda l, j: (0, 0)),
            pl.BlockSpec((None, d, tn), lambda l, j: (l, 0, j)),
            pl.BlockSpec((None, 1, tn), lambda l, j: (l, 0, j)),
        ],
        out_specs=pl.BlockSpec((None, rows, tn), lambda l, j: (l, 0, j)),
        out_shape=jax.ShapeDtypeStruct((depth, rows, n), F32),
        compiler_params=_params(("arbitrary", "arbitrary")),
        name="ada_mod",
    )(c_all, ada_w, ada_b.reshape(depth, 1, n))


def _norm_mod_body(x_ref, g_ref, sc_ref, sh_ref, o_ref):
    x = x_ref[...]
    y = x * lax.rsqrt(jnp.mean(x * x, axis=-1, keepdims=True) + EPS) * g_ref[...]
    o_ref[...] = (y * (1.0 + sc_ref[...]) + sh_ref[...]).astype(o_ref.dtype)


def _norm_body(x_ref, g_ref, o_ref):
    x = x_ref[...]
    o_ref[...] = (x * lax.rsqrt(jnp.mean(x * x, axis=-1, keepdims=True) + EPS) * g_ref[...]).astype(o_ref.dtype)


def _row_or_batch_spec(arr, tr, tn, rows_per_batch):
    if arr.ndim == 3:
        tiles_per_batch = rows_per_batch // tr
        return pl.BlockSpec((None, 1, tn), lambda i, j: (i // tiles_per_batch, 0, j))
    return pl.BlockSpec((tr, tn), lambda i, j: (i, j))


def _norm_mod(x, g, scale, shift, rows_per_batch, tr):
    m, d = x.shape
    return pl.pallas_call(
        _norm_mod_body,
        grid=(m // tr, 1),
        in_specs=[
            pl.BlockSpec((tr, d), lambda i, j: (i, 0)),
            pl.BlockSpec((1, d), lambda i, j: (0, 0)),
            _row_or_batch_spec(scale, tr, d, rows_per_batch),
            _row_or_batch_spec(shift, tr, d, rows_per_batch),
        ],
        out_specs=pl.BlockSpec((tr, d), lambda i, j: (i, 0)),
        out_shape=jax.ShapeDtypeStruct((m, d), BF16),
        compiler_params=_params(("arbitrary", "arbitrary")),
        name="norm_mod",
    )(x, g.reshape(1, d), scale, shift)


def _final_norm(x, g, tr):
    m, d = x.shape
    return pl.pallas_call(
        _norm_body,
        grid=(m // tr,),
        in_specs=[pl.BlockSpec((tr, d), lambda i: (i, 0)), pl.BlockSpec((1, d), lambda i: (0, 0))],
        out_specs=pl.BlockSpec((tr, d), lambda i: (i, 0)),
        out_shape=jax.ShapeDtypeStruct((m, d), F32),
        compiler_params=_params(("arbitrary",)),
        name="final_norm",
    )(x, g.reshape(1, d))


def _mm_tiles(m, k, n, out_bytes_per_elem):
    for tm, tn in ((1024, 1024), (1024, 512), (512, 512), (512, 256), (256, 256)):
        tm_, tn_ = min(tm, m), min(tn, n)
        if m % tm_ or n % tn_:
            continue
        need = 2 * (tm_ * k * 2 + k * tn_ * 2 + tm_ * tn_ * out_bytes_per_elem)
        if need <= MM_TILE_BUDGET:
            return tm_, tn_
    raise ValueError(f"no matmul tiling for {(m, k, n)}")


def _mm_plain_body(a_ref, w_ref, *o_refs):
    acc = jnp.dot(a_ref[...], w_ref[...], preferred_element_type=F32)
    for o_ref in o_refs:
        o_ref[...] = acc.astype(o_ref.dtype)


def _mm_tail_body(a_ref, w_ref, o_ref, tail_ref):
    acc = jnp.dot(a_ref[...], w_ref[...], preferred_element_type=F32)
    o_ref[...] = acc.astype(o_ref.dtype)
    tail_ref[...] = acc[acc.shape[0] - SUBLANES:, :]


def _mm_kv_body(a_ref, w_ref, stacked_ref, o32_ref, o16_ref):
    del stacked_ref
    acc = jnp.dot(a_ref[...], w_ref[...], preferred_element_type=F32)
    o32_ref[...] = acc
    o16_ref[...] = acc.astype(o16_ref.dtype)


def _mm_logsig_body(a_ref, w_ref, b_ref, o_ref):
    acc = jnp.dot(a_ref[...], w_ref[...], preferred_element_type=F32) + b_ref[...]
    o_ref[...] = -_softplus(-acc)


def _mm_resid_body(a_ref, w_ref, x_ref, g_ref, o_ref):
    acc = jnp.dot(a_ref[...], w_ref[...], preferred_element_type=F32)
    o_ref[...] = x_ref[...] + g_ref[...] * acc


def _mm_specs(a, layer, col0, tm, tn):
    k = a.shape[1]
    off = col0 // tn
    return [pl.BlockSpec((tm, k), lambda i, j: (i, 0)),
            pl.BlockSpec((None, k, tn), lambda i, j: (layer, 0, j + off))]


def _mm_plain(a, w, layer, col0, n, out_dtypes, name):
    m, k = a.shape
    tm, tn = _mm_tiles(m, k, n, sum(jnp.dtype(dt).itemsize for dt in out_dtypes))
    return pl.pallas_call(
        _mm_plain_body,
        grid=(m // tm, n // tn),
        in_specs=_mm_specs(a, layer, col0, tm, tn),
        out_specs=[pl.BlockSpec((tm, tn), lambda i, j: (i, j)) for _ in out_dtypes],
        out_shape=[jax.ShapeDtypeStruct((m, n), dt) for dt in out_dtypes],
        compiler_params=_params(("arbitrary", "arbitrary")),
        name=name,
    )(a, w)


def _mm_tail(a, w, layer, col0, n, rows_per_batch, name):
    m, k = a.shape
    tm, tn = _mm_tiles(rows_per_batch, k, n, 2)
    tiles_per_batch = rows_per_batch // tm
    prod, tails = pl.pallas_call(
        _mm_tail_body,
        grid=(m // tm, n // tn),
        in_specs=_mm_specs(a, layer, col0, tm, tn),
        out_specs=[pl.BlockSpec((tm, tn), lambda i, j: (i, j)),
                   pl.BlockSpec((None, SUBLANES, tn), lambda i, j: (i, 0, j))],
        out_shape=[jax.ShapeDtypeStruct((m, n), BF16),
                   jax.ShapeDtypeStruct((m // tm, SUBLANES, n), F32)],
        compiler_params=_params(("arbitrary", "arbitrary")),
        name=name,
    )(a, w)
    return prod, tails[tiles_per_batch - 1::tiles_per_batch]


def _mm_kv(a, w, layer, col0, n, stacked, name):
    m, k = a.shape
    tm, tn = _mm_tiles(m, k, n, 6)
    return pl.pallas_call(
        _mm_kv_body,
        grid=(m // tm, n // tn),
        in_specs=_mm_specs(a, layer, col0, tm, tn) + [pl.BlockSpec(memory_space=pl.ANY)],
        out_specs=[pl.BlockSpec((None, tm, tn), lambda i, j: (layer, i, j)),
                   pl.BlockSpec((tm, tn), lambda i, j: (i, j))],
        out_shape=[jax.ShapeDtypeStruct(stacked.shape, F32), jax.ShapeDtypeStruct((m, n), BF16)],
        input_output_aliases={2: 0},
        compiler_params=_params(("arbitrary", "arbitrary")),
        name=name,
    )(a, w, stacked)


def _mm_logsig(a, w, bias, name):
    m, k = a.shape
    n = w.shape[-1]
    tm = min(m, 1024)
    return pl.pallas_call(
        _mm_logsig_body,
        grid=(m // tm, 1),
        in_specs=_mm_specs(a, 0, 0, tm, n) + [pl.BlockSpec((1, n), lambda i, j: (0, 0))],
        out_specs=pl.BlockSpec((tm, n), lambda i, j: (i, 0)),
        out_shape=jax.ShapeDtypeStruct((m, n), F32),
        compiler_params=_params(("arbitrary", "arbitrary")),
        name=name,
    )(a, w, bias)


def _mm_resid(a, w, layer, x, gate, rows_per_batch, name):
    m, n = x.shape
    k = a.shape[1]
    tm, tn = _mm_tiles(min(m, rows_per_batch) if gate.ndim == 3 else m, k, n, 8)
    return pl.pallas_call(
        _mm_resid_body,
        grid=(m // tm, n // tn),
        in_specs=_mm_specs(a, layer, 0, tm, tn) + [
            pl.BlockSpec((tm, tn), lambda i, j: (i, j)),
            _row_or_batch_spec(gate, tm, tn, rows_per_batch),
        ],
        out_specs=pl.BlockSpec((tm, tn), lambda i, j: (i, j)),
        out_shape=jax.ShapeDtypeStruct((m, n), F32),
        compiler_params=_params(("arbitrary", "arbitrary")),
        name=name,
    )(a, w, x, gate)


def _cumsum_body(x_ref, f_ref, carry_ref):
    @pl.when(pl.program_id(1) == 0)
    def _():
        carry_ref[...] = jnp.zeros_like(carry_ref)

    f = jnp.dot(_tri(PAGE, True), x_ref[...], precision=lax.Precision.HIGHEST,
                preferred_element_type=F32) + carry_ref[0:1, :]
    f_ref[...] = f
    carry_ref[...] = jnp.broadcast_to(f[PAGE - 1:PAGE, :], carry_ref.shape)


def _time_cumsum(x):
    b, t, h = x.shape
    spec = pl.BlockSpec((None, PAGE, h), lambda i, p: (i, p, 0))
    return pl.pallas_call(
        _cumsum_body,
        grid=(b, t // PAGE),
        in_specs=[spec],
        out_specs=spec,
        out_shape=jax.ShapeDtypeStruct(x.shape, F32),
        scratch_shapes=[pltpu.VMEM((SUBLANES, h), F32)],
        compiler_params=_params(("arbitrary", "arbitrary")),
        name="logf_cumsum",
    )(x)


def _fox_prompt_body(q_ref, k_ref, v_ref, z_ref, f_ref, o_ref, m_ref, l_ref, acc_ref, fq_ref, *, tq, tk):
    t = q_ref.shape[0]
    qk_scale = FOX_HEAD_DIM ** -0.5 * LOG2E
    nrep = tk // LANES
    for qi in range(t // tq):
        q0 = qi * tq
        q = q_ref[q0:q0 + tq, :]
        fq_ref[...] = jnp.broadcast_to(f_ref[qi] * LOG2E, (LANES, tq)).T
        m_ref[...] = jnp.full_like(m_ref, -jnp.inf)
        l_ref[...] = jnp.zeros_like(l_ref)
        acc_ref[...] = jnp.zeros_like(acc_ref)

        def kv_step(j, k0, masked):
            kb = k_ref[pl.ds(k0, tk), :]
            vb = v_ref[pl.ds(k0, tk), :]
            s = lax.dot_general(q, kb, (((1,), (1,)), ((), ())), preferred_element_type=F32) * qk_scale
            s = s - f_ref[j] * LOG2E
            if masked:
                r = lax.broadcasted_iota(jnp.int32, (tq, tk), 0)
                c = lax.broadcasted_iota(jnp.int32, (tq, tk), 1)
                s = jnp.where(c <= r, s, -jnp.inf)
            fq = fq_ref[...]
            m_prev = m_ref[...]
            m_next = jnp.maximum(m_prev, jnp.max(s, axis=1, keepdims=True) + fq)
            p = jnp.exp2(s + _lane_tile(fq - m_next, nrep))
            alpha = jnp.exp2(m_prev - m_next)
            l_ref[...] = alpha * l_ref[...] + jnp.sum(p, axis=1, keepdims=True)
            acc_ref[...] = alpha * acc_ref[...] + jnp.dot(p.astype(BF16), vb, preferred_element_type=F32)
            m_ref[...] = m_next

        for j in range(qi):
            kv_step(j, j * tk, False)
        kv_step(qi, q0, True)
        o = acc_ref[...] / l_ref[...]
        zb = z_ref[q0:q0 + tq, :].astype(F32)
        o_ref[q0:q0 + tq, :] = (o * _silu(zb)).astype(o_ref.dtype)


def _fox_prompt_attn(q, k, v, z, f_t, batch, t, tq=512, tk=512):
    assert tq == tk and t % tq == 0
    hd = FOX_HEAD_DIM
    nk = t // tk
    f_t = f_t.reshape(batch, FOX_HEADS, nk, 1, tk)
    body = functools.partial(_fox_prompt_body, tq=tq, tk=tk)
    row_spec = pl.BlockSpec((t, hd), lambda b, h: (b, h))
    return pl.pallas_call(
        body,
        grid=(batch, FOX_HEADS),
        in_specs=[row_spec, row_spec, row_spec, row_spec,
                  pl.BlockSpec((None, None, nk, 1, tk), lambda b, h: (b, h, 0, 0, 0))],
        out_specs=row_spec,
        out_shape=jax.ShapeDtypeStruct(q.shape, BF16),
        scratch_shapes=[pltpu.VMEM((tq, LANES), F32), pltpu.VMEM((tq, LANES), F32),
                        pltpu.VMEM((tq, hd), F32), pltpu.VMEM((tq, LANES), F32)],
        compiler_params=_params(("arbitrary", "arbitrary")),
        name="fox_prompt_attn",
    )(q, k, v, z, f_t)


QPAD = 16
DECODE_HEAD_GROUP = 8


def _fox_decode_body(tbl_ref, q_ref, k_hbm, v_hbm, lf_ref, ln_ref, lnt_ref, kn_ref, vn_ref, z_ref, o_ref,
                     kbuf, vbuf, sem, m_ref, l_ref, acc_ref, fq_ref, carry_ref, *, n_new, layer):
    b = pl.program_id(0)
    p = pl.program_id(1)
    n_pages = pl.num_programs(1)
    step = b * n_pages + p
    slot = step % 2
    scale = FOX_HEAD_DIM ** -0.5
    hd = FOX_HEAD_DIM
    grows = DECODE_HEAD_GROUP * QPAD

    def page_copies(bb, pp, sl):
        page = tbl_ref[bb, n_pages - 1 - pp]
        cps = []
        for h in range(FOX_HEADS):
            cps.append(pltpu.make_async_copy(k_hbm.at[layer, page, :, h, :], kbuf.at[sl, h], sem.at[sl]))
            cps.append(pltpu.make_async_copy(v_hbm.at[layer, page, :, h, :], vbuf.at[sl, h], sem.at[sl]))
        return cps

    @pl.when(step == 0)
    def _():
        for cp in page_copies(b, p, slot):
            cp.start()

    @pl.when(step + 1 < pl.num_programs(0) * n_pages)
    def _():
        wrap = p + 1 == n_pages
        for cp in page_copies(jnp.where(wrap, b + 1, b), jnp.where(wrap, 0, p + 1), 1 - slot):
            cp.start()

    def update(load_k, load_v, key_bias, masked):
        for g in range(FOX_HEADS // DECODE_HEAD_GROUP):
            heads = range(g * DECODE_HEAD_GROUP, (g + 1) * DECODE_HEAD_GROUP)
            r0 = g * grows
            s_parts = []
            for h in heads:
                qh = q_ref[:, h * hd:(h + 1) * hd]
                s_h = lax.dot_general(qh, load_k(h), (((1,), (1,)), ((), ())), preferred_element_type=F32)
                s_parts.append(s_h * scale + jnp.broadcast_to(key_bias[h:h + 1, :], (QPAD, PAGE)))
            s = jnp.concatenate(s_parts, axis=0) + fq_ref[r0:r0 + grows, :]
            if masked:
                r = lax.broadcasted_iota(jnp.int32, (grows, PAGE), 0) & (QPAD - 1)
                c = lax.broadcasted_iota(jnp.int32, (grows, PAGE), 1)
                s = jnp.where((c <= r) & (c < n_new), s, -jnp.inf)
            m_prev = m_ref[r0:r0 + grows, :]
            m_next = jnp.maximum(m_prev, jnp.max(s, axis=1, keepdims=True))
            pr = jnp.exp(s - m_next)
            alpha = jnp.exp(m_prev - m_next)
            l_ref[r0:r0 + grows, :] = alpha * l_ref[r0:r0 + grows, :] + jnp.sum(pr, axis=1, keepdims=True)
            pb = pr.astype(BF16)
            pv = [jnp.dot(pb[i * QPAD:(i + 1) * QPAD, :], load_v(h), preferred_element_type=F32)
                  for i, h in enumerate(heads)]
            acc_ref[r0:r0 + grows, :] = alpha * acc_ref[r0:r0 + grows, :] + jnp.concatenate(pv, axis=0)
            m_ref[r0:r0 + grows, :] = m_next

    @pl.when(p == 0)
    def _():
        m_ref[...] = jnp.full_like(m_ref, -jnp.inf)
        l_ref[...] = jnp.zeros_like(l_ref)
        acc_ref[...] = jnp.zeros_like(acc_ref)
        carry_ref[...] = jnp.zeros_like(carry_ref)
        ln = ln_ref[...]
        row = lax.broadcasted_iota(jnp.int32, ln.shape, 0)
        cn = jnp.zeros_like(ln)
        for i in range(n_new):
            cn = cn + jnp.where(row >= i, ln[i:i + 1, :], 0.0)
        for h in range(FOX_HEADS):
            fq_ref[h * QPAD:(h + 1) * QPAD, :] = jnp.broadcast_to(cn[:, h:h + 1], (QPAD, LANES))
        cnt = jnp.dot(lnt_ref[...], _tri(PAGE, False), precision=lax.Precision.HIGHEST,
                      preferred_element_type=F32)
        update(lambda h: kn_ref[:, h * hd:(h + 1) * hd],
               lambda h: vn_ref[:, h * hd:(h + 1) * hd],
               -cnt, True)

    lft = lf_ref[...]
    suffix = jnp.dot(lft, 1.0 - _tri(PAGE, False), precision=lax.Precision.HIGHEST,
                     preferred_element_type=F32) + carry_ref[...]
    carry_ref[...] = carry_ref[...] + jnp.sum(lft, axis=1, keepdims=True)
    for cp in page_copies(b, p, slot):
        cp.wait()
    update(lambda h: kbuf[slot, h].astype(BF16), lambda h: vbuf[slot, h].astype(BF16), suffix, False)

    @pl.when(p == pl.num_programs(1) - 1)
    def _():
        o = acc_ref[...] / l_ref[...]
        for h in range(FOX_HEADS):
            zb = z_ref[:, h * hd:(h + 1) * hd].astype(F32)
            o_ref[:, h * hd:(h + 1) * hd] = (o[h * QPAD:(h + 1) * QPAD, :] * _silu(zb)).astype(o_ref.dtype)


def _fox_decode_attn(table, layer, q16, cache_k, cache_v, logf_t, ln16, ln_t, k_new, v_new, z16, n_new):
    b, n_pages = table.shape
    width = FOX_HEADS * FOX_HEAD_DIM
    rows = FOX_HEADS * QPAD
    body = functools.partial(_fox_decode_body, n_new=n_new, layer=layer)
    per_b = lambda shape: pl.BlockSpec((None,) + shape, lambda i, p, tbl: (i, 0, 0))
    hbm = pl.BlockSpec(memory_space=pl.ANY)
    page_buf = pltpu.VMEM((2, FOX_HEADS, PAGE, FOX_HEAD_DIM), F32)
    return pl.pallas_call(
        body,
        grid_spec=pltpu.PrefetchScalarGridSpec(
            num_scalar_prefetch=1,
            grid=(b, n_pages),
            in_specs=[
                per_b((QPAD, width)), hbm, hbm,
                pl.BlockSpec((None, FOX_HEADS, PAGE), lambda i, p, tbl: (tbl[i, n_pages - 1 - p], 0, 0)),
                per_b((QPAD, FOX_HEADS)), per_b((FOX_HEADS, PAGE)),
                per_b((PAGE, width)), per_b((PAGE, width)),
                per_b((QPAD, width)),
            ],
            out_specs=per_b((QPAD, width)),
            scratch_shapes=[page_buf, page_buf, pltpu.SemaphoreType.DMA((2,)),
                            pltpu.VMEM((rows, LANES), F32), pltpu.VMEM((rows, LANES), F32),
                            pltpu.VMEM((rows, FOX_HEAD_DIM), F32), pltpu.VMEM((rows, LANES), F32),
                            pltpu.VMEM((FOX_HEADS, LANES), F32)],
        ),
        out_shape=jax.ShapeDtypeStruct((b, QPAD, width), BF16),
        compiler_params=_params(("arbitrary", "arbitrary")),
        name="fox_decode_attn",
    )(table, q16, cache_k, cache_v, logf_t, ln16, ln_t, k_new, v_new, z16)


def _conv_body(x_ref, init_ref, w_ref, b_ref, o_ref, carry_ref):
    @pl.when(pl.program_id(2) == 0)
    def _():
        carry_ref[...] = init_ref[...]

    x = x_ref[...].astype(F32)
    tt = x.shape[0]
    c8 = carry_ref[...]
    row = lax.broadcasted_iota(jnp.int32, c8.shape, 0)
    acc = b_ref[...] + x * w_ref[SSD_CONV - 1:SSD_CONV, :]
    for shift in range(1, SSD_CONV):
        rolled = pltpu.roll(x, shift, 0)
        top = jnp.where(row < shift, pltpu.roll(c8, shift, 0), rolled[0:SUBLANES, :])
        shifted = jnp.concatenate([top, rolled[SUBLANES:, :]], axis=0)
        acc = acc + shifted * w_ref[SSD_CONV - 1 - shift:SSD_CONV - shift, :]
    o_ref[...] = _silu(acc).astype(o_ref.dtype)
    carry_ref[...] = x[tt - SUBLANES:, :]


def _conv_silu(x, init, w, bias, batch, t, tt, tc=2048):
    c = x.shape[1]
    nt = t // tt
    return pl.pallas_call(
        _conv_body,
        grid=(batch, c // tc, nt),
        in_specs=[
            pl.BlockSpec((tt, tc), lambda b, j, i: (b * nt + i, j)),
            pl.BlockSpec((None, SUBLANES, tc), lambda b, j, i: (b, 0, j)),
            pl.BlockSpec((SSD_CONV, tc), lambda b, j, i: (0, j)),
            pl.BlockSpec((1, tc), lambda b, j, i: (0, j)),
        ],
        out_specs=pl.BlockSpec((tt, tc), lambda b, j, i: (b * nt + i, j)),
        out_shape=jax.ShapeDtypeStruct(x.shape, BF16),
        scratch_shapes=[pltpu.VMEM((SUBLANES, tc), F32)],
        compiler_params=_params(("arbitrary", "arbitrary", "arbitrary")),
        name="ssd_conv_silu",
    )(x, init, w, bias.reshape(1, c))


def _ssd_body(*refs, valid, has_init):
    if has_init:
        (x_ref, b_ref, c_ref, dt_ref, dtt_ref, z_ref, bias_ref, biast_ref, alog_ref, alogt_ref,
         d_ref, nw_ref, s0_ref, y_ref, sout_ref, st_ref, yacc_ref) = refs
    else:
        (x_ref, b_ref, c_ref, dt_ref, dtt_ref, z_ref, bias_ref, biast_ref, alog_ref, alogt_ref,
         d_ref, nw_ref, y_ref, sout_ref, st_ref, yacc_ref) = refs
        s0_ref = None
    ci = pl.program_id(2)
    L = SSD_CHUNK
    npairs = SSD_HEADS_PER_GROUP // 2

    @pl.when(ci == 0)
    def _():
        if has_init:
            for k in range(npairs):
                st_ref[:, k * LANES:(k + 1) * LANES] = s0_ref[k * LANES:(k + 1) * LANES, :].T
        else:
            st_ref[...] = jnp.zeros_like(st_ref)

    dt = _softplus(dt_ref[...] + bias_ref[...])
    dtt = _softplus(dtt_ref[...] + biast_ref[...])
    if valid < L:
        dt = jnp.where(lax.broadcasted_iota(jnp.int32, dt.shape, 0) < valid, dt, 0.0)
        dtt = jnp.where(lax.broadcasted_iota(jnp.int32, dtt.shape, 1) < valid, dtt, 0.0)
    acum = jnp.dot(_tri(L, True), dt * (-jnp.exp(alog_ref[...])),
                   precision=lax.Precision.HIGHEST, preferred_element_type=F32) * LOG2E
    acumt = jnp.dot(dtt * (-jnp.exp(alogt_ref[...])), _tri(L, False),
                    precision=lax.Precision.HIGHEST, preferred_element_type=F32) * LOG2E
    last_t = acumt[:, L - 1:L]
    wt = jnp.exp2(last_t - acumt) * dtt
    chunk_decay = jnp.broadcast_to(jnp.exp2(last_t), (SSD_HEADS_PER_GROUP, LANES))
    src_term = acumt - jnp.log2(dtt)

    bm = b_ref[...]
    cm = c_ref[...]
    cb = lax.dot_general(cm, bm, (((1,), (1,)), ((), ())), preferred_element_type=F32)
    cm32 = cm.astype(F32)
    bmt = bm.astype(F32).T
    causal = lax.broadcasted_iota(jnp.int32, (L, L), 0) >= lax.broadcasted_iota(jnp.int32, (L, L), 1)
    lane = lax.broadcasted_iota(jnp.int32, (L, LANES), 1)
    first_head = lane < SSD_HEAD_DIM
    first_head_row = lax.broadcasted_iota(jnp.int32, (1, LANES), 1) < SSD_HEAD_DIM

    for k in range(npairs):
        xp = x_ref[:, k * LANES:(k + 1) * LANES]
        sp = st_ref[:, k * LANES:(k + 1) * LANES]
        rhs = jnp.concatenate([xp, sp.astype(BF16)], axis=0)
        ys, ups = [], []
        for r in (2 * k, 2 * k + 1):
            a_col = jnp.broadcast_to(acum[:, r:r + 1], (L, L))
            m = cb * jnp.exp2(jnp.where(causal, a_col - src_term[r:r + 1, :], -jnp.inf))
            ec = cm32 * jnp.exp2(a_col)
            lhs = jnp.concatenate([m, ec], axis=1).astype(BF16)
            ys.append(jnp.dot(lhs, rhs, preferred_element_type=F32))
            bw = (bmt * wt[r:r + 1, :]).astype(BF16)
            ups.append(jnp.dot(bw, xp, preferred_element_type=F32))
        yacc_ref[:, k * LANES:(k + 1) * LANES] = jnp.where(first_head, ys[0], ys[1])
        dec_row = jnp.where(first_head_row, chunk_decay[2 * k:2 * k + 1, :],
                            chunk_decay[2 * k + 1:2 * k + 2, :])
        st_ref[:, k * LANES:(k + 1) * LANES] = sp * dec_row + jnp.where(first_head, ups[0], ups[1])

    y = yacc_ref[...] + d_ref[...] * x_ref[...].astype(F32)
    y = y * _silu(z_ref[...].astype(F32))
    y = y * lax.rsqrt(jnp.mean(y * y, axis=-1, keepdims=True) + EPS) * nw_ref[...]
    y_ref[...] = y.astype(y_ref.dtype)

    @pl.when(ci == pl.num_programs(2) - 1)
    def _():
        for k in range(npairs):
            sout_ref[k * LANES:(k + 1) * LANES, :] = st_ref[:, k * LANES:(k + 1) * LANES].T


def _ssd_scan(act, dt_g, dtt_g, z, dt_bias, a_log, d_skip, norm_w, state0, batch, t, valid):
    L = SSD_CHUNK
    nc = t // L
    gw = SSD_GROUP_WIDTH
    hpg = SSD_HEADS_PER_GROUP
    g = SSD_GROUPS
    inner = g * gw
    b_blk0 = inner // SSD_STATE
    c_blk0 = b_blk0 + g
    has_init = state0 is not None
    body = functools.partial(_ssd_body, valid=valid, has_init=has_init)
    grp = lambda shape: pl.BlockSpec((None,) + shape, lambda b, gi, c: (gi, 0, 0))
    in_specs = [
        pl.BlockSpec((L, gw), lambda b, gi, c: (b * nc + c, gi)),
        pl.BlockSpec((L, SSD_STATE), lambda b, gi, c: (b * nc + c, b_blk0 + gi)),
        pl.BlockSpec((L, SSD_STATE), lambda b, gi, c: (b * nc + c, c_blk0 + gi)),
        pl.BlockSpec((None, L, hpg), lambda b, gi, c: (gi, b * nc + c, 0)),
        pl.BlockSpec((None, hpg, L), lambda b, gi, c: (gi, 0, b * nc + c)),
        pl.BlockSpec((L, gw), lambda b, gi, c: (b * nc + c, gi)),
        grp((1, hpg)), grp((hpg, 1)), grp((1, hpg)), grp((hpg, 1)),
        grp((1, gw)), grp((1, gw)),
    ]
    args = [act, act, act, dt_g, dtt_g, z,
            dt_bias.reshape(g, 1, hpg), dt_bias.reshape(g, hpg, 1),
            a_log.reshape(g, 1, hpg), a_log.reshape(g, hpg, 1),
            jnp.repeat(d_skip, SSD_HEAD_DIM).reshape(g, 1, gw), norm_w.reshape(g, 1, gw)]
    state_spec = pl.BlockSpec((None, gw, SSD_STATE), lambda b, gi, c: (b, gi, 0))
    if has_init:
        in_specs.append(state_spec)
        args.append(state0.reshape(batch, g * gw, SSD_STATE))
    y, s_out = pl.pallas_call(
        body,
        grid=(batch, g, nc),
        in_specs=in_specs,
        out_specs=[pl.BlockSpec((L, gw), lambda b, gi, c: (b * nc + c, gi)), state_spec],
        out_shape=[jax.ShapeDtypeStruct((batch * t, inner), BF16),
                   jax.ShapeDtypeStruct((batch, g * gw, SSD_STATE), F32)],
        scratch_shapes=[pltpu.VMEM((SSD_STATE, gw), F32), pltpu.VMEM((L, gw), F32)],
        compiler_params=_params(("arbitrary", "arbitrary", "arbitrary")),
        name="ssd_scan",
    )(*args)
    return y, s_out.reshape(batch, g * hpg, SSD_HEAD_DIM, SSD_STATE)


def _mod_parts(mod_l, nb_prompt, nb_sample, rows_per_sample, d):
    def prompt(a):
        return a[:nb_prompt].reshape(nb_prompt, 1, d)

    def sample(a):
        return jnp.repeat(a[nb_prompt:nb_prompt + nb_sample], rows_per_sample, axis=0)

    parts = [mod_l[:, i * d:(i + 1) * d] for i in range(3)]
    return [prompt(a) for a in parts], [sample(a) for a in parts]


def _pad_rows(a, batch, t, t_pad):
    c = a.shape[-1]
    a = a.reshape(batch, t, c)
    return jnp.pad(a, ((0, 0), (0, t_pad - t), (0, 0))).reshape(batch * t_pad, c)


def kernel(x_prompt, x_sample, c_prompt, c_sample, cache_k, cache_v, cache_logf, state_ssm, state_conv,
           page_table, norm_w, ada_w, ada_b, fox_w_in, fox_b_f, fox_w_out, ssd_w_in, ssd_conv_w, ssd_conv_b,
           ssd_dt_bias, ssd_a_log, ssd_d, ssd_norm_w, ssd_w_out, final_norm_w):
    bp, t, d = x_prompt.shape
    bs, ts, _ = x_sample.shape
    depth = norm_w.shape[0]
    n_fox = fox_w_in.shape[0]
    mp, ms = bp * t, bs * ts
    width = FOX_HEADS * FOX_HEAD_DIM
    inner = SSD_GROUPS * SSD_GROUP_WIDTH
    conv_dim = inner + 2 * SSD_GROUPS * SSD_STATE
    n_heads = SSD_GROUPS * SSD_HEADS_PER_GROUP

    xp = x_prompt.reshape(mp, d)
    xs = x_sample.reshape(ms, d)

    c_all = jnp.concatenate([c_prompt, c_sample], axis=0)
    c_all = jnp.pad(c_all, ((0, (-c_all.shape[0]) % SUBLANES), (0, 0)))
    mod = _ada_mod(c_all, ada_w, ada_b)

    fox_w_in16 = fox_w_in.astype(BF16)
    fox_w_out16 = fox_w_out.astype(BF16)
    ssd_w_in16 = ssd_w_in.astype(BF16)
    ssd_w_out16 = ssd_w_out.astype(BF16)
    fox_w_f16 = jnp.pad(fox_w_in[:, :, 4 * width:], ((0, 0), (0, 0), (0, LANES - FOX_HEADS))).astype(BF16)

    kp_all = jnp.zeros((n_fox, mp, width), F32)
    vp_all = jnp.zeros((n_fox, mp, width), F32)
    outs ={name: [] for name in ("fp", "ks", "vs", "fs", "sp", "cp", "ss", "cs")}
    for i in range(depth):
        j = i // 2
        (sh_p, sc_p, g_p), (sh_s, sc_s, g_s) = _mod_parts(mod[i], bp, bs, ts, d)
        hp = _norm_mod(xp, norm_w[i], sc_p, sh_p, t, 256)
        hs = _norm_mod(xs, norm_w[i], sc_s, sh_s, ts, ms)
        if i % 2 == 0:
            b_f = jnp.pad(fox_b_f[j], (0, LANES - FOX_HEADS)).reshape(1, LANES)

            def logf_of(h):
                return _mm_logsig(h, fox_w_f16[j:j + 1], b_f, "fox_logf")[:, :FOX_HEADS]

            (q,) = _mm_plain(hp, fox_w_in16, j, 0, width, [BF16], "fox_q")
            kp_all, k16 = _mm_kv(hp, fox_w_in16, j, width, width, kp_all, "fox_k")
            vp_all, v16 = _mm_kv(hp, fox_w_in16, j, 2 * width, width, vp_all, "fox_v")
            (z,) = _mm_plain(hp, fox_w_in16, j, 3 * width, width, [BF16], "fox_z")
            logf = logf_of(hp)
            f_cum = _time_cumsum(logf.reshape(bp, t, FOX_HEADS))
            og = _fox_prompt_attn(q, k16, v16, z, jnp.swapaxes(f_cum, 1, 2), bp, t, tq=min(512, t), tk=min(512, t))
            xp = _mm_resid(og, fox_w_out16, j, xp, g_p, t, "fox_out")
            outs["fp"].append(logf.reshape(bp, t, FOX_HEADS))

            (q,) = _mm_plain(hs, fox_w_in16, j, 0, width, [BF16], "fox_q_s")
            k32, k16 = _mm_plain(hs, fox_w_in16, j, width, width, [F32, BF16], "fox_k_s")
            v32, v16 = _mm_plain(hs, fox_w_in16, j, 2 * width, width, [F32, BF16], "fox_v_s")
            (z,) = _mm_plain(hs, fox_w_in16, j, 3 * width, width, [BF16], "fox_z_s")
            logf = logf_of(hs)

            def pad_q(a, rows):
                return jnp.pad(a.reshape(bs, ts, -1), ((0, 0), (0, rows - ts), (0, 0)))

            ln16 = pad_q(logf, QPAD)
            ln_t = jnp.swapaxes(pad_q(logf, PAGE), 1, 2)
            og = _fox_decode_attn(page_table, j, pad_q(q, QPAD), cache_k, cache_v,
                                  jnp.swapaxes(cache_logf[j], 1, 2), ln16, ln_t,
                                  pad_q(k16, PAGE), pad_q(v16, PAGE), pad_q(z, QPAD), ts)
            og = og[:, :ts].reshape(ms, width)
            xs = _mm_resid(og, fox_w_out16, j, xs, g_s, ts, "fox_out_s")
            outs["ks"].append(k32.reshape(bs, ts, FOX_HEADS, FOX_HEAD_DIM))
            outs["vs"].append(v32.reshape(bs, ts, FOX_HEADS, FOX_HEAD_DIM))
            outs["fs"].append(logf.reshape(bs, ts, FOX_HEADS))
        else:
            def mixer(h, x_res, gate, batch, tt_, conv_prev, state0, suffix):
                m = batch * tt_
                (zz,) = _mm_plain(h, ssd_w_in16, j, 0, inner, [BF16], "ssd_z" + suffix)
                (dt_raw,) = _mm_plain(h, ssd_w_in16, j, inner + conv_dim, n_heads, [F32], "ssd_dt" + suffix)
                t_pad = -(-tt_ // SSD_CHUNK) * SSD_CHUNK
                if conv_prev is None:
                    xbc, tail = _mm_tail(h, ssd_w_in16, j, inner, conv_dim, tt_, "ssd_xbc" + suffix)
                    init = jnp.zeros((batch, SUBLANES, conv_dim), F32)
                    new_conv = tail[:, SUBLANES - (SSD_CONV - 1):]
                else:
                    (xbc,) = _mm_plain(h, ssd_w_in16, j, inner, conv_dim, [F32], "ssd_xbc" + suffix)
                    init = jnp.pad(conv_prev, ((0, 0), (SUBLANES - (SSD_CONV - 1), 0), (0, 0)))
                    full = jnp.concatenate([conv_prev, xbc.reshape(batch, tt_, conv_dim)], axis=1)
                    new_conv = full[:, tt_:]
                if t_pad != tt_:
                    zz, xbc, dt_raw = (_pad_rows(a, batch, tt_, t_pad) for a in (zz, xbc, dt_raw))
                act = _conv_silu(xbc, init, ssd_conv_w[j], ssd_conv_b[j], batch, t_pad, min(512, t_pad))
                dt_g = dt_raw.reshape(batch * t_pad, SSD_GROUPS, SSD_HEADS_PER_GROUP).transpose(1, 0, 2)
                dtt_g = dt_g.transpose(0, 2, 1)
                y, new_state = _ssd_scan(act, dt_g, dtt_g, zz, ssd_dt_bias[j], ssd_a_log[j], ssd_d[j],
                                         ssd_norm_w[j], state0, batch, t_pad, min(tt_, SSD_CHUNK))
                if t_pad != tt_:
                    y = y.reshape(batch, t_pad, inner)[:, :tt_].reshape(m, inner)
                x_new = _mm_resid(y, ssd_w_out16, j, x_res, gate, tt_, "ssd_out" + suffix)
                return x_new, new_conv, new_state

            xp, cvp, smp = mixer(hp, xp, g_p, bp, t, None, None, "")
            xs, cvs, sms = mixer(hs, xs, g_s, bs, ts, state_conv[j], state_ssm[j], "_s")
            outs["sp"].append(smp)
            outs["cp"].append(cvp)
            outs["ss"].append(sms)
            outs["cs"].append(cvs)

    y_prompt = _final_norm(xp, final_norm_w, 256).reshape(bp, t, d)
    y_sample = _final_norm(xs, final_norm_w, ms).reshape(bs, ts, d)
    kv_shape = (n_fox, bp, t, FOX_HEADS, FOX_HEAD_DIM)
    return (y_prompt, y_sample,
            kp_all.reshape(kv_shape), vp_all.reshape(kv_shape), jnp.stack(outs["fp"]),
            jnp.stack(outs["ks"]), jnp.stack(outs["vs"]), jnp.stack(outs["fs"]),
            jnp.stack(outs["sp"]), jnp.stack(outs["cp"]), jnp.stack(outs["ss"]), jnp.stack(outs["cs"]))
```

```python
import functools
import math

import jax
import jax.numpy as jnp
from jax import lax
from jax.experimental import pallas as pl
from jax.experimental.pallas import tpu as pltpu

F32 = jnp.float32
BF16 = jnp.bfloat16

EPS = 1e-6
LOG2E = math.log2(math.e)
LANES = 128
SUBLANES = 8
PAGE = 128
FOX_HEADS = 32
FOX_HEAD_DIM = 128
SSD_GROUPS = 8
SSD_HEAD_DIM = 64
SSD_STATE = 128
SSD_HEADS_PER_GROUP = 16
SSD_GROUP_WIDTH = SSD_HEADS_PER_GROUP * SSD_HEAD_DIM
SSD_CHUNK = 128
SSD_CONV = 4
V7X_VMEM_BYTES = 64 * 1024 * 1024
VMEM_LIMIT = V7X_VMEM_BYTES * 7 // 8
MM_TILE_BUDGET = 48 * 1024 * 1024


def _params(semantics):
    return pltpu.CompilerParams(dimension_semantics=semantics, vmem_limit_bytes=VMEM_LIMIT)


def _silu(x):
    return x * (1.0 / (1.0 + jnp.exp2(x * -LOG2E)))


def _softplus(x):
    return jnp.maximum(x, 0.0) + jnp.log1p(jnp.exp(-jnp.abs(x)))


def _tri(n, lower):
    r = lax.broadcasted_iota(jnp.int32, (n, n), 0)
    c = lax.broadcasted_iota(jnp.int32, (n, n), 1)
    return ((r >= c) if lower else (r <= c)).astype(F32)


def _lane_tile(x, reps):
    return x if reps == 1 else jnp.concatenate([x] * reps, axis=1)


def _ada_body(c_ref, w_ref, b_ref, o_ref):
    a = _silu(c_ref[...]).astype(BF16)
    o_ref[...] = jnp.dot(a, w_ref[...].astype(BF16), preferred_element_type=F32) + b_ref[...]


def _ada_mod(c_all, ada_w, ada_b, tn=512):
    depth, d, n = ada_w.shape
    rows = c_all.shape[0]
    return pl.pallas_call(
        _ada_body,
        grid=(depth, n // tn),
        in_specs=[
            pl.BlockSpec((rows, d), lambda l, j: (0, 0)),
            pl.BlockSpec((None, d, tn), lambda l, j: (l, 0, j)),
            pl.BlockSpec((None, 1, tn), lambda l, j: (l, 0, j)),
        ],
        out_specs=pl.BlockSpec((None, rows, tn), lambda l, j: (l, 0, j)),
        out_shape=jax.ShapeDtypeStruct((depth, rows, n), F32),
        compiler_params=_params(("arbitrary", "arbitrary")),
        name="ada_mod",
    )(c_all, ada_w, ada_b.reshape(depth, 1, n))


def _norm_mod_body(x_ref, g_ref, sc_ref, sh_ref, o_ref):
    x = x_ref[...]
    y = x * lax.rsqrt(jnp.mean(x * x, axis=-1, keepdims=True) + EPS) * g_ref[...]
    o_ref[...] = (y * (1.0 + sc_ref[...]) + sh_ref[...]).astype(o_ref.dtype)


def _norm_body(x_ref, g_ref, o_ref):
    x = x_ref[...]
    o_ref[...] = (x * lax.rsqrt(jnp.mean(x * x, axis=-1, keepdims=True) + EPS) * g_ref[...]).astype(o_ref.dtype)


def _row_or_batch_spec(arr, tr, tn, rows_per_batch):
    if arr.ndim == 3:
        tiles_per_batch = rows_per_batch // tr
        return pl.BlockSpec((None, 1, tn), lambda i, j: (i // tiles_per_batch, 0, j))
    return pl.BlockSpec((tr, tn), lambda i, j: (i, j))


def _norm_mod(x, g, scale, shift, rows_per_batch, tr):
    m, d = x.shape
    return pl.pallas_call(
        _norm_mod_body,
        grid=(m // tr, 1),
        in_specs=[
            pl.BlockSpec((tr, d), lambda i, j: (i, 0)),
            pl.BlockSpec((1, d), lambda i, j: (0, 0)),
            _row_or_batch_spec(scale, tr, d, rows_per_batch),
            _row_or_batch_spec(shift, tr, d, rows_per_batch),
        ],
        out_specs=pl.BlockSpec((tr, d), lambda i, j: (i, 0)),
        out_shape=jax.ShapeDtypeStruct((m, d), BF16),
        compiler_params=_params(("arbitrary", "arbitrary")),
        name="norm_mod",
    )(x, g.reshape(1, d), scale, shift)


def _final_norm(x, g, tr):
    m, d = x.shape
    return pl.pallas_call(
        _norm_body,
        grid=(m // tr,),
        in_specs=[pl.BlockSpec((tr, d), lambda i: (i, 0)), pl.BlockSpec((1, d), lambda i: (0, 0))],
        out_specs=pl.BlockSpec((tr, d), lambda i: (i, 0)),
        out_shape=jax.ShapeDtypeStruct((m, d), F32),
        compiler_params=_params(("arbitrary",)),
        name="final_norm",
    )(x, g.reshape(1, d))


def _mm_tiles(m, k, n, out_bytes_per_elem):
    for tm, tn in ((1024, 1024), (1024, 512), (512, 512), (512, 256), (256, 256)):
        tm_, tn_ = min(tm, m), min(tn, n)
        if m % tm_ or n % tn_:
            continue
        need = 2 * (tm_ * k * 2 + k * tn_ * 2 + tm_ * tn_ * out_bytes_per_elem)
        if need <= MM_TILE_BUDGET:
            return tm_, tn_
    raise ValueError(f"no matmul tiling for {(m, k, n)}")


def _mm_plain_body(a_ref, w_ref, *o_refs):
    acc = jnp.dot(a_ref[...], w_ref[...], preferred_element_type=F32)
    for o_ref in o_refs:
        o_ref[...] = acc.astype(o_ref.dtype)


def _causal_conv_silu(x, prev, w_ref, b_ref):
    row = lax.broadcasted_iota(jnp.int32, prev.shape, 0)
    acc = b_ref[...] + x * w_ref[SSD_CONV - 1:SSD_CONV, :]
    for shift in range(1, SSD_CONV):
        rolled = pltpu.roll(x, shift, 0)
        top = jnp.where(row < shift, pltpu.roll(prev, shift, 0), rolled[0:SUBLANES, :])
        shifted = jnp.concatenate([top, rolled[SUBLANES:, :]], axis=0)
        acc = acc + shifted * w_ref[SSD_CONV - 1 - shift:SSD_CONV - shift, :]
    return _silu(acc)


def _mm_conv_body(a_ref, w_ref, cw_ref, cb_ref, o_ref, tail_ref, carry_ref, *, tiles_per_batch):
    i = pl.program_id(0)
    j = pl.program_id(1)
    acc = jnp.dot(a_ref[...], w_ref[...], preferred_element_type=F32)
    last = acc[acc.shape[0] - SUBLANES:, :]
    tail_ref[...] = last

    @pl.when(i % tiles_per_batch == 0)
    def _():
        carry_ref[j] = jnp.zeros_like(last)

    o_ref[...] = _causal_conv_silu(acc, carry_ref[j], cw_ref, cb_ref).astype(o_ref.dtype)
    carry_ref[j] = last


def _mm_kv_body(a_ref, w_ref, stacked_ref, o32_ref, o16_ref):
    del stacked_ref
    acc = jnp.dot(a_ref[...], w_ref[...], preferred_element_type=F32)
    o32_ref[...] = acc
    o16_ref[...] = acc.astype(o16_ref.dtype)


def _mm_logsig_body(a_ref, w_ref, b_ref, o_ref):
    acc = jnp.dot(a_ref[...], w_ref[...], preferred_element_type=F32) + b_ref[...]
    o_ref[...] = -_softplus(-acc)


def _mm_resid_body(a_ref, w_ref, x_ref, g_ref, o_ref):
    acc = jnp.dot(a_ref[...], w_ref[...], preferred_element_type=F32)
    o_ref[...] = x_ref[...] + g_ref[...] * acc


def _mm_specs(a, layer, col0, tm, tn):
    k = a.shape[1]
    off = col0 // tn
    return [pl.BlockSpec((tm, k), lambda i, j: (i, 0)),
            pl.BlockSpec((None, k, tn), lambda i, j: (layer, 0, j + off))]


def _mm_plain(a, w, layer, col0, n, out_dtypes, name):
    m, k = a.shape
    tm, tn = _mm_tiles(m, k, n, sum(jnp.dtype(dt).itemsize for dt in out_dtypes))
    return pl.pallas_call(
        _mm_plain_body,
        grid=(m // tm, n // tn),
        in_specs=_mm_specs(a, layer, col0, tm, tn),
        out_specs=[pl.BlockSpec((tm, tn), lambda i, j: (i, j)) for _ in out_dtypes],
        out_shape=[jax.ShapeDtypeStruct((m, n), dt) for dt in out_dtypes],
        compiler_params=_params(("arbitrary", "arbitrary")),
        name=name,
    )(a, w)


def _mm_conv(a, w, layer, col0, n, rows_per_batch, conv_w, conv_b, name):
    m, k = a.shape
    tm, tn = _mm_tiles(rows_per_batch, k, n, 2)
    tiles_per_batch = rows_per_batch // tm
    body = functools.partial(_mm_conv_body, tiles_per_batch=tiles_per_batch)
    act, tails = pl.pallas_call(
        body,
        grid=(m // tm, n // tn),
        in_specs=_mm_specs(a, layer, col0, tm, tn) + [
            pl.BlockSpec((SSD_CONV, tn), lambda i, j: (0, j)),
            pl.BlockSpec((1, tn), lambda i, j: (0, j)),
        ],
        out_specs=[pl.BlockSpec((tm, tn), lambda i, j: (i, j)),
                   pl.BlockSpec((None, SUBLANES, tn), lambda i, j: (i, 0, j))],
        out_shape=[jax.ShapeDtypeStruct((m, n), BF16),
                   jax.ShapeDtypeStruct((m // tm, SUBLANES, n), F32)],
        scratch_shapes=[pltpu.VMEM((n // tn, SUBLANES, tn), F32)],
        compiler_params=_params(("arbitrary", "arbitrary")),
        name=name,
    )(a, w, conv_w, conv_b.reshape(1, n))
    return act, tails[tiles_per_batch - 1::tiles_per_batch]


def _mm_kv(a, w, layer, col0, n, stacked, name):
    m, k = a.shape
    tm, tn = _mm_tiles(m, k, n, 6)
    return pl.pallas_call(
        _mm_kv_body,
        grid=(m // tm, n // tn),
        in_specs=_mm_specs(a, layer, col0, tm, tn) + [pl.BlockSpec(memory_space=pl.ANY)],
        out_specs=[pl.BlockSpec((None, tm, tn), lambda i, j: (layer, i, j)),
                   pl.BlockSpec((tm, tn), lambda i, j: (i, j))],
        out_shape=[jax.ShapeDtypeStruct(stacked.shape, F32), jax.ShapeDtypeStruct((m, n), BF16)],
        input_output_aliases={2: 0},
        compiler_params=_params(("arbitrary", "arbitrary")),
        name=name,
    )(a, w, stacked)


def _mm_logsig(a, w, bias, name):
    m, k = a.shape
    n = w.shape[-1]
    tm = min(m, 1024)
    return pl.pallas_call(
        _mm_logsig_body,
        grid=(m // tm, 1),
        in_specs=_mm_specs(a, 0, 0, tm, n) + [pl.BlockSpec((1, n), lambda i, j: (0, 0))],
        out_specs=pl.BlockSpec((tm, n), lambda i, j: (i, 0)),
        out_shape=jax.ShapeDtypeStruct((m, n), F32),
        compiler_params=_params(("arbitrary", "arbitrary")),
        name=name,
    )(a, w, bias)


def _mm_resid(a, w, layer, x, gate, rows_per_batch, name):
    m, n = x.shape
    k = a.shape[1]
    tm, tn = _mm_tiles(min(m, rows_per_batch) if gate.ndim == 3 else m, k, n, 8)
    return pl.pallas_call(
        _mm_resid_body,
        grid=(m // tm, n // tn),
        in_specs=_mm_specs(a, layer, 0, tm, tn) + [
            pl.BlockSpec((tm, tn), lambda i, j: (i, j)),
            _row_or_batch_spec(gate, tm, tn, rows_per_batch),
        ],
        out_specs=pl.BlockSpec((tm, tn), lambda i, j: (i, j)),
        out_shape=jax.ShapeDtypeStruct((m, n), F32),
        compiler_params=_params(("arbitrary", "arbitrary")),
        name=name,
    )(a, w, x, gate)


def _cumsum_body(x_ref, f_ref, carry_ref):
    @pl.when(pl.program_id(1) == 0)
    def _():
        carry_ref[...] = jnp.zeros_like(carry_ref)

    f = jnp.dot(_tri(PAGE, True), x_ref[...], precision=lax.Precision.HIGHEST,
                preferred_element_type=F32) + carry_ref[0:1, :]
    f_ref[...] = f
    carry_ref[...] = jnp.broadcast_to(f[PAGE - 1:PAGE, :], carry_ref.shape)


def _time_cumsum(x):
    b, t, h = x.shape
    spec = pl.BlockSpec((None, PAGE, h), lambda i, p: (i, p, 0))
    return pl.pallas_call(
        _cumsum_body,
        grid=(b, t // PAGE),
        in_specs=[spec],
        out_specs=spec,
        out_shape=jax.ShapeDtypeStruct(x.shape, F32),
        scratch_shapes=[pltpu.VMEM((SUBLANES, h), F32)],
        compiler_params=_params(("arbitrary", "arbitrary")),
        name="logf_cumsum",
    )(x)


def _fox_prompt_body(q_ref, k_ref, v_ref, z_ref, f_ref, o_ref, m_ref, l_ref, acc_ref, fq_ref, *, tq, tk):
    t = q_ref.shape[0]
    qk_scale = FOX_HEAD_DIM ** -0.5 * LOG2E
    nrep = tk // LANES
    for qi in range(t // tq):
        q0 = qi * tq
        q = q_ref[q0:q0 + tq, :]
        fq_ref[...] = jnp.broadcast_to(f_ref[qi] * LOG2E, (LANES, tq)).T
        m_ref[...] = jnp.full_like(m_ref, -jnp.inf)
        l_ref[...] = jnp.zeros_like(l_ref)
        acc_ref[...] = jnp.zeros_like(acc_ref)

        def kv_step(j, k0, masked):
            kb = k_ref[pl.ds(k0, tk), :]
            vb = v_ref[pl.ds(k0, tk), :]
            s = lax.dot_general(q, kb, (((1,), (1,)), ((), ())), preferred_element_type=F32) * qk_scale
            s = s - f_ref[j] * LOG2E
            if masked:
                r = lax.broadcasted_iota(jnp.int32, (tq, tk), 0)
                c = lax.broadcasted_iota(jnp.int32, (tq, tk), 1)
                s = jnp.where(c <= r, s, -jnp.inf)
            fq = fq_ref[...]
            m_prev = m_ref[...]
            m_next = jnp.maximum(m_prev, jnp.max(s, axis=1, keepdims=True) + fq)
            p = jnp.exp2(s + _lane_tile(fq - m_next, nrep))
            alpha = jnp.exp2(m_prev - m_next)
            l_ref[...] = alpha * l_ref[...] + jnp.sum(p, axis=1, keepdims=True)
            acc_ref[...] = alpha * acc_ref[...] + jnp.dot(p.astype(BF16), vb, preferred_element_type=F32)
            m_ref[...] = m_next

        for j in range(qi):
            kv_step(j, j * tk, False)
        kv_step(qi, q0, True)
        o = acc_ref[...] / l_ref[...]
        zb = z_ref[q0:q0 + tq, :].astype(F32)
        o_ref[q0:q0 + tq, :] = (o * _silu(zb)).astype(o_ref.dtype)


def _fox_prompt_attn(q, k, v, z, f_t, batch, t, tq=512, tk=512):
    assert tq == tk and t % tq == 0
    hd = FOX_HEAD_DIM
    nk = t // tk
    f_t = f_t.reshape(batch, FOX_HEADS, nk, 1, tk)
    body = functools.partial(_fox_prompt_body, tq=tq, tk=tk)
    row_spec = pl.BlockSpec((t, hd), lambda b, h: (b, h))
    return pl.pallas_call(
        body,
        grid=(batch, FOX_HEADS),
        in_specs=[row_spec, row_spec, row_spec, row_spec,
                  pl.BlockSpec((None, None, nk, 1, tk), lambda b, h: (b, h, 0, 0, 0))],
        out_specs=row_spec,
        out_shape=jax.ShapeDtypeStruct(q.shape, BF16),
        scratch_shapes=[pltpu.VMEM((tq, LANES), F32), pltpu.VMEM((tq, LANES), F32),
                        pltpu.VMEM((tq, hd), F32), pltpu.VMEM((tq, LANES), F32)],
        compiler_params=_params(("arbitrary", "arbitrary")),
        name="fox_prompt_attn",
    )(q, k, v, z, f_t)


QPAD = 16
DECODE_HEAD_GROUP = FOX_HEADS


def _fox_decode_body(tbl_ref, q_ref, k_hbm, v_hbm, lf_ref, ln_ref, lnt_ref, kn_ref, vn_ref, z_ref, o_ref,
                     kbuf, vbuf, sem, m_ref, l_ref, acc_ref, fq_ref, carry_ref, *, n_new, layer):
    b = pl.program_id(0)
    p = pl.program_id(1)
    n_pages = pl.num_programs(1)
    step = b * n_pages + p
    slot = step % 2
    scale = FOX_HEAD_DIM ** -0.5
    hd = FOX_HEAD_DIM
    grows = DECODE_HEAD_GROUP * QPAD

    def page_copies(bb, pp, sl):
        page = tbl_ref[bb, n_pages - 1 - pp]
        cps = []
        for h in range(FOX_HEADS):
            cps.append(pltpu.make_async_copy(k_hbm.at[layer, page, :, h, :], kbuf.at[sl, h], sem.at[sl]))
            cps.append(pltpu.make_async_copy(v_hbm.at[layer, page, :, h, :], vbuf.at[sl, h], sem.at[sl]))
        return cps

    @pl.when(step == 0)
    def _():
        for cp in page_copies(b, p, slot):
            cp.start()

    @pl.when(step + 1 < pl.num_programs(0) * n_pages)
    def _():
        wrap = p + 1 == n_pages
        for cp in page_copies(jnp.where(wrap, b + 1, b), jnp.where(wrap, 0, p + 1), 1 - slot):
            cp.start()

    def update(load_k, load_v, key_bias, masked):
        for g in range(FOX_HEADS // DECODE_HEAD_GROUP):
            heads = range(g * DECODE_HEAD_GROUP, (g + 1) * DECODE_HEAD_GROUP)
            r0 = g * grows
            s_parts = []
            for h in heads:
                qh = q_ref[:, h * hd:(h + 1) * hd]
                s_h = lax.dot_general(qh, load_k(h), (((1,), (1,)), ((), ())), preferred_element_type=F32)
                s_parts.append(s_h * scale + jnp.broadcast_to(key_bias[h:h + 1, :], (QPAD, PAGE)))
            s = jnp.concatenate(s_parts, axis=0) + fq_ref[r0:r0 + grows, :]
            if masked:
                r = lax.broadcasted_iota(jnp.int32, (grows, PAGE), 0) & (QPAD - 1)
                c = lax.broadcasted_iota(jnp.int32, (grows, PAGE), 1)
                s = jnp.where((c <= r) & (c < n_new), s, -jnp.inf)
            m_prev = m_ref[r0:r0 + grows, :]
            m_next = jnp.maximum(m_prev, jnp.max(s, axis=1, keepdims=True))
            pr = jnp.exp(s - m_next)
            alpha = jnp.exp(m_prev - m_next)
            l_ref[r0:r0 + grows, :] = alpha * l_ref[r0:r0 + grows, :] + jnp.sum(pr, axis=1, keepdims=True)
            pb = pr.astype(BF16)
            pv = [jnp.dot(pb[i * QPAD:(i + 1) * QPAD, :], load_v(h), preferred_element_type=F32)
                  for i, h in enumerate(heads)]
            acc_ref[r0:r0 + grows, :] = alpha * acc_ref[r0:r0 + grows, :] + jnp.concatenate(pv, axis=0)
            m_ref[r0:r0 + grows, :] = m_next

    @pl.when(p == 0)
    def _():
        m_ref[...] = jnp.full_like(m_ref, -jnp.inf)
        l_ref[...] = jnp.zeros_like(l_ref)
        acc_ref[...] = jnp.zeros_like(acc_ref)
        carry_ref[...] = jnp.zeros_like(carry_ref)
        ln = ln_ref[...]
        row = lax.broadcasted_iota(jnp.int32, ln.shape, 0)
        cn = jnp.zeros_like(ln)
        for i in range(n_new):
            cn = cn + jnp.where(row >= i, ln[i:i + 1, :], 0.0)
        for h in range(FOX_HEADS):
            fq_ref[h * QPAD:(h + 1) * QPAD, :] = jnp.broadcast_to(cn[:, h:h + 1], (QPAD, LANES))
        cnt = jnp.dot(lnt_ref[...], _tri(PAGE, False), precision=lax.Precision.HIGHEST,
                      preferred_element_type=F32)
        update(lambda h: kn_ref[:, h * hd:(h + 1) * hd],
               lambda h: vn_ref[:, h * hd:(h + 1) * hd],
               -cnt, True)

    lft = lf_ref[...]
    suffix = jnp.dot(lft, 1.0 - _tri(PAGE, False), precision=lax.Precision.HIGHEST,
                     preferred_element_type=F32) + carry_ref[...]
    carry_ref[...] = carry_ref[...] + jnp.sum(lft, axis=1, keepdims=True)
    for cp in page_copies(b, p, slot):
        cp.wait()
    update(lambda h: kbuf[slot, h].astype(BF16), lambda h: vbuf[slot, h].astype(BF16), suffix, False)

    @pl.when(p == pl.num_programs(1) - 1)
    def _():
        o = acc_ref[...] / l_ref[...]
        for h in range(FOX_HEADS):
            zb = z_ref[:, h * hd:(h + 1) * hd].astype(F32)
            o_ref[:, h * hd:(h + 1) * hd] = (o[h * QPAD:(h + 1) * QPAD, :] * _silu(zb)).astype(o_ref.dtype)


def _fox_decode_attn(table, layer, q16, cache_k, cache_v, logf_t, ln16, ln_t, k_new, v_new, z16, n_new):
    b, n_pages = table.shape
    width = FOX_HEADS * FOX_HEAD_DIM
    rows = FOX_HEADS * QPAD
    body = functools.partial(_fox_decode_body, n_new=n_new, layer=layer)
    per_b = lambda shape: pl.BlockSpec((None,) + shape, lambda i, p, tbl: (i, 0, 0))
    hbm = pl.BlockSpec(memory_space=pl.ANY)
    page_buf = pltpu.VMEM((2, FOX_HEADS, PAGE, FOX_HEAD_DIM), F32)
    return pl.pallas_call(
        body,
        grid_spec=pltpu.PrefetchScalarGridSpec(
            num_scalar_prefetch=1,
            grid=(b, n_pages),
            in_specs=[
                per_b((QPAD, width)), hbm, hbm,
                pl.BlockSpec((None, FOX_HEADS, PAGE), lambda i, p, tbl: (tbl[i, n_pages - 1 - p], 0, 0)),
                per_b((QPAD, FOX_HEADS)), per_b((FOX_HEADS, PAGE)),
                per_b((PAGE, width)), per_b((PAGE, width)),
                per_b((QPAD, width)),
            ],
            out_specs=per_b((QPAD, width)),
            scratch_shapes=[page_buf, page_buf, pltpu.SemaphoreType.DMA((2,)),
                            pltpu.VMEM((rows, LANES), F32), pltpu.VMEM((rows, LANES), F32),
                            pltpu.VMEM((rows, FOX_HEAD_DIM), F32), pltpu.VMEM((rows, LANES), F32),
                            pltpu.VMEM((FOX_HEADS, LANES), F32)],
        ),
        out_shape=jax.ShapeDtypeStruct((b, QPAD, width), BF16),
        compiler_params=_params(("arbitrary", "arbitrary")),
        name="fox_decode_attn",
    )(table, q16, cache_k, cache_v, logf_t, ln16, ln_t, k_new, v_new, z16)


def _conv_body(x_ref, init_ref, w_ref, b_ref, o_ref, carry_ref):
    @pl.when(pl.program_id(2) == 0)
    def _():
        carry_ref[...] = init_ref[...]

    x = x_ref[...].astype(F32)
    o_ref[...] = _causal_conv_silu(x, carry_ref[...], w_ref, b_ref).astype(o_ref.dtype)
    carry_ref[...] = x[x.shape[0] - SUBLANES:, :]


def _conv_silu(x, init, w, bias, batch, t, tt, tc=2048):
    c = x.shape[1]
    nt = t // tt
    return pl.pallas_call(
        _conv_body,
        grid=(batch, c // tc, nt),
        in_specs=[
            pl.BlockSpec((tt, tc), lambda b, j, i: (b * nt + i, j)),
            pl.BlockSpec((None, SUBLANES, tc), lambda b, j, i: (b, 0, j)),
            pl.BlockSpec((SSD_CONV, tc), lambda b, j, i: (0, j)),
            pl.BlockSpec((1, tc), lambda b, j, i: (0, j)),
        ],
        out_specs=pl.BlockSpec((tt, tc), lambda b, j, i: (b * nt + i, j)),
        out_shape=jax.ShapeDtypeStruct(x.shape, BF16),
        scratch_shapes=[pltpu.VMEM((SUBLANES, tc), F32)],
        compiler_params=_params(("arbitrary", "arbitrary", "arbitrary")),
        name="ssd_conv_silu",
    )(x, init, w, bias.reshape(1, c))


def _ssd_body(*refs, valid, has_init):
    if has_init:
        (x_ref, b_ref, c_ref, dt_ref, dtt_ref, z_ref, bias_ref, biast_ref, alog_ref, alogt_ref,
         d_ref, nw_ref, s0_ref, y_ref, sout_ref, st_ref, yacc_ref) = refs
    else:
        (x_ref, b_ref, c_ref, dt_ref, dtt_ref, z_ref, bias_ref, biast_ref, alog_ref, alogt_ref,
         d_ref, nw_ref, y_ref, sout_ref, st_ref, yacc_ref) = refs
        s0_ref = None
    ci = pl.program_id(2)
    L = SSD_CHUNK
    npairs = SSD_HEADS_PER_GROUP // 2

    @pl.when(ci == 0)
    def _():
        if has_init:
            for k in range(npairs):
                st_ref[:, k * LANES:(k + 1) * LANES] = s0_ref[k * LANES:(k + 1) * LANES, :].T
        else:
            st_ref[...] = jnp.zeros_like(st_ref)

    dt = _softplus(dt_ref[...] + bias_ref[...])
    dtt = _softplus(dtt_ref[...] + biast_ref[...])
    if valid < L:
        dt = jnp.where(lax.broadcasted_iota(jnp.int32, dt.shape, 0) < valid, dt, 0.0)
        dtt = jnp.where(lax.broadcasted_iota(jnp.int32, dtt.shape, 1) < valid, dtt, 0.0)
    acum = jnp.dot(_tri(L, True), dt * (-jnp.exp(alog_ref[...])),
                   precision=lax.Precision.HIGHEST, preferred_element_type=F32) * LOG2E
    acumt = jnp.dot(dtt * (-jnp.exp(alogt_ref[...])), _tri(L, False),
                    precision=lax.Precision.HIGHEST, preferred_element_type=F32) * LOG2E
    last_t = acumt[:, L - 1:L]
    wt = jnp.exp2(last_t - acumt) * dtt
    chunk_decay = jnp.broadcast_to(jnp.exp2(last_t), (SSD_HEADS_PER_GROUP, LANES))
    src_term = acumt - jnp.log2(dtt)

    bm = b_ref[...]
    cm = c_ref[...]
    cb = lax.dot_general(cm, bm, (((1,), (1,)), ((), ())), preferred_element_type=F32)
    cm32 = cm.astype(F32)
    bmt = bm.astype(F32).T
    causal = lax.broadcasted_iota(jnp.int32, (L, L), 0) >= lax.broadcasted_iota(jnp.int32, (L, L), 1)
    lane = lax.broadcasted_iota(jnp.int32, (L, LANES), 1)
    first_head = lane < SSD_HEAD_DIM
    first_head_row = lax.broadcasted_iota(jnp.int32, (1, LANES), 1) < SSD_HEAD_DIM

    for k in range(npairs):
        xp = x_ref[:, k * LANES:(k + 1) * LANES]
        sp = st_ref[:, k * LANES:(k + 1) * LANES]
        rhs = jnp.concatenate([xp, sp.astype(BF16)], axis=0)
        ys, ups = [], []
        for r in (2 * k, 2 * k + 1):
            a_col = jnp.broadcast_to(acum[:, r:r + 1], (L, L))
            m = cb * jnp.exp2(jnp.where(causal, a_col - src_term[r:r + 1, :], -jnp.inf))
            ec = cm32 * jnp.exp2(a_col)
            lhs = jnp.concatenate([m, ec], axis=1).astype(BF16)
            ys.append(jnp.dot(lhs, rhs, preferred_element_type=F32))
            bw = (bmt * wt[r:r + 1, :]).astype(BF16)
            ups.append(jnp.dot(bw, xp, preferred_element_type=F32))
        yacc_ref[:, k * LANES:(k + 1) * LANES] = jnp.where(first_head, ys[0], ys[1])
        dec_row = jnp.where(first_head_row, chunk_decay[2 * k:2 * k + 1, :],
                            chunk_decay[2 * k + 1:2 * k + 2, :])
        st_ref[:, k * LANES:(k + 1) * LANES] = sp * dec_row + jnp.where(first_head, ups[0], ups[1])

    y = yacc_ref[...] + d_ref[...] * x_ref[...].astype(F32)
    y = y * _silu(z_ref[...].astype(F32))
    y = y * lax.rsqrt(jnp.mean(y * y, axis=-1, keepdims=True) + EPS) * nw_ref[...]
    y_ref[...] = y.astype(y_ref.dtype)

    @pl.when(ci == pl.num_programs(2) - 1)
    def _():
        for k in range(npairs):
            sout_ref[k * LANES:(k + 1) * LANES, :] = st_ref[:, k * LANES:(k + 1) * LANES].T


def _ssd_scan(act, dt_g, dtt_g, z, dt_bias, a_log, d_skip, norm_w, state0, batch, t, valid):
    L = SSD_CHUNK
    nc = t // L
    gw = SSD_GROUP_WIDTH
    hpg = SSD_HEADS_PER_GROUP
    g = SSD_GROUPS
    inner = g * gw
    b_blk0 = inner // SSD_STATE
    c_blk0 = b_blk0 + g
    has_init = state0 is not None
    body = functools.partial(_ssd_body, valid=valid, has_init=has_init)
    grp = lambda shape: pl.BlockSpec((None,) + shape, lambda b, gi, c: (gi, 0, 0))
    in_specs = [
        pl.BlockSpec((L, gw), lambda b, gi, c: (b * nc + c, gi)),
        pl.BlockSpec((L, SSD_STATE), lambda b, gi, c: (b * nc + c, b_blk0 + gi)),
        pl.BlockSpec((L, SSD_STATE), lambda b, gi, c: (b * nc + c, c_blk0 + gi)),
        pl.BlockSpec((None, L, hpg), lambda b, gi, c: (gi, b * nc + c, 0)),
        pl.BlockSpec((None, hpg, L), lambda b, gi, c: (gi, 0, b * nc + c)),
        pl.BlockSpec((L, gw), lambda b, gi, c: (b * nc + c, gi)),
        grp((1, hpg)), grp((hpg, 1)), grp((1, hpg)), grp((hpg, 1)),
        grp((1, gw)), grp((1, gw)),
    ]
    args = [act, act, act, dt_g, dtt_g, z,
            dt_bias.reshape(g, 1, hpg), dt_bias.reshape(g, hpg, 1),
            a_log.reshape(g, 1, hpg), a_log.reshape(g, hpg, 1),
            jnp.repeat(d_skip, SSD_HEAD_DIM).reshape(g, 1, gw), norm_w.reshape(g, 1, gw)]
    state_spec = pl.BlockSpec((None, gw, SSD_STATE), lambda b, gi, c: (b, gi, 0))
    if has_init:
        in_specs.append(state_spec)
        args.append(state0.reshape(batch, g * gw, SSD_STATE))
    y, s_out = pl.pallas_call(
        body,
        grid=(batch, g, nc),
        in_specs=in_specs,
        out_specs=[pl.BlockSpec((L, gw), lambda b, gi, c: (b * nc + c, gi)), state_spec],
        out_shape=[jax.ShapeDtypeStruct((batch * t, inner), BF16),
                   jax.ShapeDtypeStruct((batch, g * gw, SSD_STATE), F32)],
        scratch_shapes=[pltpu.VMEM((SSD_STATE, gw), F32), pltpu.VMEM((L, gw), F32)],
        compiler_params=_params(("arbitrary", "arbitrary", "arbitrary")),
        name="ssd_scan",
    )(*args)
    return y, s_out.reshape(batch, g * hpg, SSD_HEAD_DIM, SSD_STATE)


def _mod_parts(mod_l, nb_prompt, nb_sample, rows_per_sample, d):
    def prompt(a):
        return a[:nb_prompt].reshape(nb_prompt, 1, d)

    def sample(a):
        return jnp.repeat(a[nb_prompt:nb_prompt + nb_sample], rows_per_sample, axis=0)

    parts = [mod_l[:, i * d:(i + 1) * d] for i in range(3)]
    return [prompt(a) for a in parts], [sample(a) for a in parts]


def _pad_rows(a, batch, t, t_pad):
    c = a.shape[-1]
    a = a.reshape(batch, t, c)
    return jnp.pad(a, ((0, 0), (0, t_pad - t), (0, 0))).reshape(batch * t_pad, c)


def kernel(x_prompt, x_sample, c_prompt, c_sample, cache_k, cache_v, cache_logf, state_ssm, state_conv,
           page_table, norm_w, ada_w, ada_b, fox_w_in, fox_b_f, fox_w_out, ssd_w_in, ssd_conv_w, ssd_conv_b,
           ssd_dt_bias, ssd_a_log, ssd_d, ssd_norm_w, ssd_w_out, final_norm_w):
    bp, t, d = x_prompt.shape
    bs, ts, _ = x_sample.shape
    depth = norm_w.shape[0]
    n_fox = fox_w_in.shape[0]
    mp, ms = bp * t, bs * ts
    width = FOX_HEADS * FOX_HEAD_DIM
    inner = SSD_GROUPS * SSD_GROUP_WIDTH
    conv_dim = inner + 2 * SSD_GROUPS * SSD_STATE
    n_heads = SSD_GROUPS * SSD_HEADS_PER_GROUP

    xp = x_prompt.reshape(mp, d)
    xs = x_sample.reshape(ms, d)

    c_all = jnp.concatenate([c_prompt, c_sample], axis=0)
    c_all = jnp.pad(c_all, ((0, (-c_all.shape[0]) % SUBLANES), (0, 0)))
    mod = _ada_mod(c_all, ada_w, ada_b)

    fox_w_in16 = fox_w_in.astype(BF16)
    fox_w_out16 = fox_w_out.astype(BF16)
    ssd_w_in16 = ssd_w_in.astype(BF16)
    ssd_w_out16 = ssd_w_out.astype(BF16)
    fox_w_f16 = jnp.pad(fox_w_in[:, :, 4 * width:], ((0, 0), (0, 0), (0, LANES - FOX_HEADS))).astype(BF16)

    kp_all = jnp.zeros((n_fox, mp, width), F32)
    vp_all = jnp.zeros((n_fox, mp, width), F32)
    outs ={name: [] for name in ("fp", "ks", "vs", "fs", "sp", "cp", "ss", "cs")}
    for i in range(depth):
        j = i // 2
        (sh_p, sc_p, g_p), (sh_s, sc_s, g_s) = _mod_parts(mod[i], bp, bs, ts, d)
        hp = _norm_mod(xp, norm_w[i], sc_p, sh_p, t, 256)
        hs = _norm_mod(xs, norm_w[i], sc_s, sh_s, ts, ms)
        if i % 2 == 0:
            b_f = jnp.pad(fox_b_f[j], (0, LANES - FOX_HEADS)).reshape(1, LANES)

            def logf_of(h):
                return _mm_logsig(h, fox_w_f16[j:j + 1], b_f, "fox_logf")[:, :FOX_HEADS]

            (q,) = _mm_plain(hp, fox_w_in16, j, 0, width, [BF16], "fox_q")
            kp_all, k16 = _mm_kv(hp, fox_w_in16, j, width, width, kp_all, "fox_k")
            vp_all, v16 = _mm_kv(hp, fox_w_in16, j, 2 * width, width, vp_all, "fox_v")
            (z,) = _mm_plain(hp, fox_w_in16, j, 3 * width, width, [BF16], "fox_z")
            logf = logf_of(hp)
            f_cum = _time_cumsum(logf.reshape(bp, t, FOX_HEADS))
            og = _fox_prompt_attn(q, k16, v16, z, jnp.swapaxes(f_cum, 1, 2), bp, t, tq=min(512, t), tk=min(512, t))
            xp = _mm_resid(og, fox_w_out16, j, xp, g_p, t, "fox_out")
            outs["fp"].append(logf.reshape(bp, t, FOX_HEADS))

            (q,) = _mm_plain(hs, fox_w_in16, j, 0, width, [BF16], "fox_q_s")
            k32, k16 = _mm_plain(hs, fox_w_in16, j, width, width, [F32, BF16], "fox_k_s")
            v32, v16 = _mm_plain(hs, fox_w_in16, j, 2 * width, width, [F32, BF16], "fox_v_s")
            (z,) = _mm_plain(hs, fox_w_in16, j, 3 * width, width, [BF16], "fox_z_s")
            logf = logf_of(hs)

            def pad_q(a, rows):
                return jnp.pad(a.reshape(bs, ts, -1), ((0, 0), (0, rows - ts), (0, 0)))

            ln16 = pad_q(logf, QPAD)
            ln_t = jnp.swapaxes(pad_q(logf, PAGE), 1, 2)
            og = _fox_decode_attn(page_table, j, pad_q(q, QPAD), cache_k, cache_v,
                                  jnp.swapaxes(cache_logf[j], 1, 2), ln16, ln_t,
                                  pad_q(k16, PAGE), pad_q(v16, PAGE), pad_q(z, QPAD), ts)
            og = og[:, :ts].reshape(ms, width)
            xs = _mm_resid(og, fox_w_out16, j, xs, g_s, ts, "fox_out_s")
            outs["ks"].append(k32.reshape(bs, ts, FOX_HEADS, FOX_HEAD_DIM))
            outs["vs"].append(v32.reshape(bs, ts, FOX_HEADS, FOX_HEAD_DIM))
            outs["fs"].append(logf.reshape(bs, ts, FOX_HEADS))
        else:
            def mixer(h, x_res, gate, batch, tt_, conv_prev, state0, suffix):
                m = batch * tt_
                (zz,) = _mm_plain(h, ssd_w_in16, j, 0, inner, [BF16], "ssd_z" + suffix)
                (dt_raw,) = _mm_plain(h, ssd_w_in16, j, inner + conv_dim, n_heads, [F32], "ssd_dt" + suffix)
                t_pad = -(-tt_ // SSD_CHUNK) * SSD_CHUNK
                if conv_prev is None:
                    assert t_pad == tt_
                    act, tail = _mm_conv(h, ssd_w_in16, j, inner, conv_dim, tt_, ssd_conv_w[j], ssd_conv_b[j],
                                         "ssd_xbc_conv" + suffix)
                    new_conv = tail[:, SUBLANES - (SSD_CONV - 1):]
                else:
                    (xbc,) = _mm_plain(h, ssd_w_in16, j, inner, conv_dim, [F32], "ssd_xbc" + suffix)
                    init = jnp.pad(conv_prev, ((0, 0), (SUBLANES - (SSD_CONV - 1), 0), (0, 0)))
                    full = jnp.concatenate([conv_prev, xbc.reshape(batch, tt_, conv_dim)], axis=1)
                    new_conv = full[:, tt_:]
                    if t_pad != tt_:
                        xbc = _pad_rows(xbc, batch, tt_, t_pad)
                    act = _conv_silu(xbc, init, ssd_conv_w[j], ssd_conv_b[j], batch, t_pad, min(512, t_pad))
                if t_pad != tt_:
                    zz, dt_raw = (_pad_rows(a, batch, tt_, t_pad) for a in (zz, dt_raw))
                dt_g = dt_raw.reshape(batch * t_pad, SSD_GROUPS, SSD_HEADS_PER_GROUP).transpose(1, 0, 2)
                dtt_g = dt_g.transpose(0, 2, 1)
                y, new_state = _ssd_scan(act, dt_g, dtt_g, zz, ssd_dt_bias[j], ssd_a_log[j], ssd_d[j],
                                         ssd_norm_w[j], state0, batch, t_pad, min(tt_, SSD_CHUNK))
                if t_pad != tt_:
                    y = y.reshape(batch, t_pad, inner)[:, :tt_].reshape(m, inner)
                x_new = _mm_resid(y, ssd_w_out16, j, x_res, gate, tt_, "ssd_out" + suffix)
                return x_new, new_conv, new_state

            xp, cvp, smp = mixer(hp, xp, g_p, bp, t, None, None, "")
            xs, cvs, sms = mixer(hs, xs, g_s, bs, ts, state_conv[j], state_ssm[j], "_s")
            outs["sp"].append(smp)
            outs["cp"].append(cvp)
            outs["ss"].append(sms)
            outs["cs"].append(cvs)

    y_prompt = _final_norm(xp, final_norm_w, 256).reshape(bp, t, d)
    y_sample = _final_norm(xs, final_norm_w, ms).reshape(bs, ts, d)
    kv_shape = (n_fox, bp, t, FOX_HEADS, FOX_HEAD_DIM)
    return (y_prompt, y_sample,
            kp_all.reshape(kv_shape), vp_all.reshape(kv_shape), jnp.stack(outs["fp"]),
            jnp.stack(outs["ks"]), jnp.stack(outs["vs"]), jnp.stack(outs["fs"]),
            jnp.stack(outs["sp"]), jnp.stack(outs["cp"]), jnp.stack(outs["ss"]), jnp.stack(outs["cs"]))
```

```python
import functools
import math

import jax
import jax.numpy as jnp
from jax import lax
from jax.experimental import pallas as pl
from jax.experimental.pallas import tpu as pltpu

F32 = jnp.float32
BF16 = jnp.bfloat16

EPS = 1e-6
LOG2E = math.log2(math.e)
LANES = 128
SUBLANES = 8
PAGE = 128
FOX_HEADS = 32
FOX_HEAD_DIM = 128
SSD_GROUPS = 8
SSD_HEAD_DIM = 64
SSD_STATE = 128
SSD_HEADS_PER_GROUP = 16
SSD_GROUP_WIDTH = SSD_HEADS_PER_GROUP * SSD_HEAD_DIM
SSD_CHUNK = 128
SSD_CONV = 4
V7X_VMEM_BYTES = 64 * 1024 * 1024
VMEM_LIMIT = V7X_VMEM_BYTES * 7 // 8
MM_TILE_BUDGET = 48 * 1024 * 1024


def _params(semantics):
    return pltpu.CompilerParams(dimension_semantics=semantics, vmem_limit_bytes=VMEM_LIMIT)


def _silu(x):
    return x * (1.0 / (1.0 + jnp.exp2(x * -LOG2E)))


def _softplus(x):
    return jnp.maximum(x, 0.0) + jnp.log1p(jnp.exp(-jnp.abs(x)))


def _tri(n, lower):
    r = lax.broadcasted_iota(jnp.int32, (n, n), 0)
    c = lax.broadcasted_iota(jnp.int32, (n, n), 1)
    return ((r >= c) if lower else (r <= c)).astype(F32)


def _lane_tile(x, reps):
    return x if reps == 1 else jnp.concatenate([x] * reps, axis=1)


def _ada_body(c_ref, w_ref, b_ref, o_ref):
    a = _silu(c_ref[...]).astype(BF16)
    o_ref[...] = jnp.dot(a, w_ref[...].astype(BF16), preferred_element_type=F32) + b_ref[...]


def _ada_mod(c_all, ada_w, ada_b, tn=512):
    depth, d, n = ada_w.shape
    rows = c_all.shape[0]
    return pl.pallas_call(
        _ada_body,
        grid=(depth, n // tn),
        in_specs=[
            pl.BlockSpec((rows, d), lambda l, j: (0, 0)),
            pl.BlockSpec((None, d, tn), lambda l, j: (l, 0, j)),
            pl.BlockSpec((None, 1, tn), lambda l, j: (l, 0, j)),
        ],
        out_specs=pl.BlockSpec((None, rows, tn), lambda l, j: (l, 0, j)),
        out_shape=jax.ShapeDtypeStruct((depth, rows, n), F32),
        compiler_params=_params(("arbitrary", "arbitrary")),
        name="ada_mod",
    )(c_all, ada_w, ada_b.reshape(depth, 1, n))


def _norm_mod_body(x_ref, g_ref, sc_ref, sh_ref, o_ref):
    x = x_ref[...]
    y = x * lax.rsqrt(jnp.mean(x * x, axis=-1, keepdims=True) + EPS) * g_ref[...]
    o_ref[...] = (y * (1.0 + sc_ref[...]) + sh_ref[...]).astype(o_ref.dtype)


def _norm_body(x_ref, g_ref, o_ref):
    x = x_ref[...]
    o_ref[...] = (x * lax.rsqrt(jnp.mean(x * x, axis=-1, keepdims=True) + EPS) * g_ref[...]).astype(o_ref.dtype)


def _row_or_batch_spec(arr, tr, tn, rows_per_batch):
    if arr.ndim == 3:
        tiles_per_batch = rows_per_batch // tr
        return pl.BlockSpec((None, 1, tn), lambda i, j: (i // tiles_per_batch, 0, j))
    return pl.BlockSpec((tr, tn), lambda i, j: (i, j))


def _norm_mod(x, g, scale, shift, rows_per_batch, tr):
    m, d = x.shape
    return pl.pallas_call(
        _norm_mod_body,
        grid=(m // tr, 1),
        in_specs=[
            pl.BlockSpec((tr, d), lambda i, j: (i, 0)),
            pl.BlockSpec((1, d), lambda i, j: (0, 0)),
            _row_or_batch_spec(scale, tr, d, rows_per_batch),
            _row_or_batch_spec(shift, tr, d, rows_per_batch),
        ],
        out_specs=pl.BlockSpec((tr, d), lambda i, j: (i, 0)),
        out_shape=jax.ShapeDtypeStruct((m, d), BF16),
        compiler_params=_params(("arbitrary", "arbitrary")),
        name="norm_mod",
    )(x, g.reshape(1, d), scale, shift)


def _final_norm(x, g, tr):
    m, d = x.shape
    return pl.pallas_call(
        _norm_body,
        grid=(m // tr,),
        in_specs=[pl.BlockSpec((tr, d), lambda i: (i, 0)), pl.BlockSpec((1, d), lambda i: (0, 0))],
        out_specs=pl.BlockSpec((tr, d), lambda i: (i, 0)),
        out_shape=jax.ShapeDtypeStruct((m, d), F32),
        compiler_params=_params(("arbitrary",)),
        name="final_norm",
    )(x, g.reshape(1, d))


def _mm_tiles(m, k, n, out_bytes_per_elem):
    for tm, tn in ((1024, 1024), (1024, 512), (512, 512), (512, 256), (256, 256)):
        tm_, tn_ = min(tm, m), min(tn, n)
        if m % tm_ or n % tn_:
            continue
        need = 2 * (tm_ * k * 2 + k * tn_ * 2 + tm_ * tn_ * out_bytes_per_elem)
        if need <= MM_TILE_BUDGET:
            return tm_, tn_
    raise ValueError(f"no matmul tiling for {(m, k, n)}")


def _mm_plain_body(a_ref, w_ref, *o_refs):
    acc = jnp.dot(a_ref[...], w_ref[...], preferred_element_type=F32)
    for o_ref in o_refs:
        o_ref[...] = acc.astype(o_ref.dtype)


def _causal_conv_silu(x, prev, w_ref, b_ref):
    row = lax.broadcasted_iota(jnp.int32, prev.shape, 0)
    acc = b_ref[...] + x * w_ref[SSD_CONV - 1:SSD_CONV, :]
    for shift in range(1, SSD_CONV):
        rolled = pltpu.roll(x, shift, 0)
        top = jnp.where(row < shift, pltpu.roll(prev, shift, 0), rolled[0:SUBLANES, :])
        shifted = jnp.concatenate([top, rolled[SUBLANES:, :]], axis=0)
        acc = acc + shifted * w_ref[SSD_CONV - 1 - shift:SSD_CONV - shift, :]
    return _silu(acc)


def _mm_conv_body(a_ref, w_ref, cw_ref, cb_ref, o_ref, tail_ref, carry_ref, *, tiles_per_batch):
    i = pl.program_id(0)
    j = pl.program_id(1)
    acc = jnp.dot(a_ref[...], w_ref[...], preferred_element_type=F32)
    last = acc[acc.shape[0] - SUBLANES:, :]
    tail_ref[...] = last

    @pl.when(i % tiles_per_batch == 0)
    def _():
        carry_ref[j] = jnp.zeros_like(last)

    o_ref[...] = _causal_conv_silu(acc, carry_ref[j], cw_ref, cb_ref).astype(o_ref.dtype)
    carry_ref[j] = last


def _mm_kv_body(a_ref, w_ref, stacked_ref, o32_ref, o16_ref):
    del stacked_ref
    acc = jnp.dot(a_ref[...], w_ref[...], preferred_element_type=F32)
    o32_ref[...] = acc
    o16_ref[...] = acc.astype(o16_ref.dtype)


def _mm_logsig_body(a_ref, w_ref, b_ref, o_ref):
    acc = jnp.dot(a_ref[...], w_ref[...], preferred_element_type=F32) + b_ref[...]
    o_ref[...] = -_softplus(-acc)


def _mm_resid_body(a_ref, w_ref, x_ref, g_ref, o_ref):
    acc = jnp.dot(a_ref[...], w_ref[...], preferred_element_type=F32)
    o_ref[...] = x_ref[...] + g_ref[...] * acc


def _mm_specs(a, layer, col0, tm, tn):
    k = a.shape[1]
    off = col0 // tn
    return [pl.BlockSpec((tm, k), lambda i, j: (i, 0)),
            pl.BlockSpec((None, k, tn), lambda i, j: (layer, 0, j + off))]


def _mm_plain(a, w, layer, col0, n, out_dtypes, name):
    m, k = a.shape
    tm, tn = _mm_tiles(m, k, n, sum(jnp.dtype(dt).itemsize for dt in out_dtypes))
    return pl.pallas_call(
        _mm_plain_body,
        grid=(m // tm, n // tn),
        in_specs=_mm_specs(a, layer, col0, tm, tn),
        out_specs=[pl.BlockSpec((tm, tn), lambda i, j: (i, j)) for _ in out_dtypes],
        out_shape=[jax.ShapeDtypeStruct((m, n), dt) for dt in out_dtypes],
        compiler_params=_params(("arbitrary", "arbitrary")),
        name=name,
    )(a, w)


def _mm_conv(a, w, layer, col0, n, rows_per_batch, conv_w, conv_b, name):
    m, k = a.shape
    tm, tn = _mm_tiles(rows_per_batch, k, n, 2)
    tiles_per_batch = rows_per_batch // tm
    body = functools.partial(_mm_conv_body, tiles_per_batch=tiles_per_batch)
    act, tails = pl.pallas_call(
        body,
        grid=(m // tm, n // tn),
        in_specs=_mm_specs(a, layer, col0, tm, tn) + [
            pl.BlockSpec((SSD_CONV, tn), lambda i, j: (0, j)),
            pl.BlockSpec((1, tn), lambda i, j: (0, j)),
        ],
        out_specs=[pl.BlockSpec((tm, tn), lambda i, j: (i, j)),
                   pl.BlockSpec((None, SUBLANES, tn), lambda i, j: (i, 0, j))],
        out_shape=[jax.ShapeDtypeStruct((m, n), BF16),
                   jax.ShapeDtypeStruct((m // tm, SUBLANES, n), F32)],
        scratch_shapes=[pltpu.VMEM((n // tn, SUBLANES, tn), F32)],
        compiler_params=_params(("arbitrary", "arbitrary")),
        name=name,
    )(a, w, conv_w, conv_b.reshape(1, n))
    return act, tails[tiles_per_batch - 1::tiles_per_batch]


def _mm_kv(a, w, layer, col0, n, stacked, name):
    m, k = a.shape
    tm, tn = _mm_tiles(m, k, n, 6)
    return pl.pallas_call(
        _mm_kv_body,
        grid=(m // tm, n // tn),
        in_specs=_mm_specs(a, layer, col0, tm, tn) + [pl.BlockSpec(memory_space=pl.ANY)],
        out_specs=[pl.BlockSpec((None, tm, tn), lambda i, j: (layer, i, j)),
                   pl.BlockSpec((tm, tn), lambda i, j: (i, j))],
        out_shape=[jax.ShapeDtypeStruct(stacked.shape, F32), jax.ShapeDtypeStruct((m, n), BF16)],
        input_output_aliases={2: 0},
        compiler_params=_params(("arbitrary", "arbitrary")),
        name=name,
    )(a, w, stacked)


def _mm_logsig(a, w, bias, name):
    m, k = a.shape
    n = w.shape[-1]
    tm = min(m, 1024)
    return pl.pallas_call(
        _mm_logsig_body,
        grid=(m // tm, 1),
        in_specs=_mm_specs(a, 0, 0, tm, n) + [pl.BlockSpec((1, n), lambda i, j: (0, 0))],
        out_specs=pl.BlockSpec((tm, n), lambda i, j: (i, 0)),
        out_shape=jax.ShapeDtypeStruct((m, n), F32),
        compiler_params=_params(("arbitrary", "arbitrary")),
        name=name,
    )(a, w, bias)


def _mm_resid(a, w, layer, x, gate, rows_per_batch, name):
    m, n = x.shape
    k = a.shape[1]
    tm, tn = _mm_tiles(min(m, rows_per_batch) if gate.ndim == 3 else m, k, n, 8)
    return pl.pallas_call(
        _mm_resid_body,
        grid=(m // tm, n // tn),
        in_specs=_mm_specs(a, layer, 0, tm, tn) + [
            pl.BlockSpec((tm, tn), lambda i, j: (i, j)),
            _row_or_batch_spec(gate, tm, tn, rows_per_batch),
        ],
        out_specs=pl.BlockSpec((tm, tn), lambda i, j: (i, j)),
        out_shape=jax.ShapeDtypeStruct((m, n), F32),
        compiler_params=_params(("arbitrary", "arbitrary")),
        name=name,
    )(a, w, x, gate)


def _cumsum_body(x_ref, f_ref, carry_ref):
    @pl.when(pl.program_id(1) == 0)
    def _():
        carry_ref[...] = jnp.zeros_like(carry_ref)

    f = jnp.dot(_tri(PAGE, True), x_ref[...], precision=lax.Precision.HIGHEST,
                preferred_element_type=F32) + carry_ref[0:1, :]
    f_ref[...] = f
    carry_ref[...] = jnp.broadcast_to(f[PAGE - 1:PAGE, :], carry_ref.shape)


def _time_cumsum(x):
    b, t, h = x.shape
    spec = pl.BlockSpec((None, PAGE, h), lambda i, p: (i, p, 0))
    return pl.pallas_call(
        _cumsum_body,
        grid=(b, t // PAGE),
        in_specs=[spec],
        out_specs=spec,
        out_shape=jax.ShapeDtypeStruct(x.shape, F32),
        scratch_shapes=[pltpu.VMEM((SUBLANES, h), F32)],
        compiler_params=_params(("arbitrary", "arbitrary")),
        name="logf_cumsum",
    )(x)


def _fox_prompt_body(q_ref, k_ref, v_ref, z_ref, f_ref, o_ref, m_ref, l_ref, acc_ref, fq_ref, *, tq, tk):
    t = q_ref.shape[0]
    qk_scale = FOX_HEAD_DIM ** -0.5 * LOG2E
    nrep = tk // LANES
    for qi in range(t // tq):
        q0 = qi * tq
        q = q_ref[q0:q0 + tq, :]
        fq_ref[...] = jnp.broadcast_to(f_ref[qi] * LOG2E, (LANES, tq)).T
        m_ref[...] = jnp.full_like(m_ref, -jnp.inf)
        l_ref[...] = jnp.zeros_like(l_ref)
        acc_ref[...] = jnp.zeros_like(acc_ref)

        def kv_step(j, k0, masked):
            kb = k_ref[pl.ds(k0, tk), :]
            vb = v_ref[pl.ds(k0, tk), :]
            s = lax.dot_general(q, kb, (((1,), (1,)), ((), ())), preferred_element_type=F32) * qk_scale
            s = s - f_ref[j] * LOG2E
            if masked:
                r = lax.broadcasted_iota(jnp.int32, (tq, tk), 0)
                c = lax.broadcasted_iota(jnp.int32, (tq, tk), 1)
                s = jnp.where(c <= r, s, -jnp.inf)
            fq = fq_ref[...]
            m_prev = m_ref[...]
            m_next = jnp.maximum(m_prev, jnp.max(s, axis=1, keepdims=True) + fq)
            p = jnp.exp2(s + _lane_tile(fq - m_next, nrep))
            alpha = jnp.exp2(m_prev - m_next)
            l_ref[...] = alpha * l_ref[...] + jnp.sum(p, axis=1, keepdims=True)
            acc_ref[...] = alpha * acc_ref[...] + jnp.dot(p.astype(BF16), vb, preferred_element_type=F32)
            m_ref[...] = m_next

        for j in range(qi):
            kv_step(j, j * tk, False)
        kv_step(qi, q0, True)
        o = acc_ref[...] / l_ref[...]
        zb = z_ref[q0:q0 + tq, :].astype(F32)
        o_ref[q0:q0 + tq, :] = (o * _silu(zb)).astype(o_ref.dtype)


def _fox_prompt_attn(q, k, v, z, f_t, batch, t, tq=512, tk=512):
    assert tq == tk and t % tq == 0
    hd = FOX_HEAD_DIM
    nk = t // tk
    f_t = f_t.reshape(batch, FOX_HEADS, nk, 1, tk)
    body = functools.partial(_fox_prompt_body, tq=tq, tk=tk)
    row_spec = pl.BlockSpec((t, hd), lambda b, h: (b, h))
    return pl.pallas_call(
        body,
        grid=(batch, FOX_HEADS),
        in_specs=[row_spec, row_spec, row_spec, row_spec,
                  pl.BlockSpec((None, None, nk, 1, tk), lambda b, h: (b, h, 0, 0, 0))],
        out_specs=row_spec,
        out_shape=jax.ShapeDtypeStruct(q.shape, BF16),
        scratch_shapes=[pltpu.VMEM((tq, LANES), F32), pltpu.VMEM((tq, LANES), F32),
                        pltpu.VMEM((tq, hd), F32), pltpu.VMEM((tq, LANES), F32)],
        compiler_params=_params(("arbitrary", "arbitrary")),
        name="fox_prompt_attn",
    )(q, k, v, z, f_t)


QPAD = 16
DECODE_HEAD_GROUP = FOX_HEADS


DECODE_PAGES_PER_STEP = 2


def _fox_decode_body(tbl_ref, q_ref, k_hbm, v_hbm, *refs, n_new, layer, n_pages):
    pps = DECODE_PAGES_PER_STEP
    lf_refs = refs[:pps]
    (ln_ref, lnt_ref, kn_ref, vn_ref, z_ref, o_ref,
     kbuf, vbuf, sem, m_ref, l_ref, acc_ref, fq_ref, carry_ref) = refs[pps:]
    b = pl.program_id(0)
    p = pl.program_id(1)
    n_steps = pl.num_programs(1)
    step = b * n_steps + p
    slot = step % 2
    scale = FOX_HEAD_DIM ** -0.5
    hd = FOX_HEAD_DIM
    grows = DECODE_HEAD_GROUP * QPAD

    def page_copies(bb, pp, sl):
        cps = []
        for u in range(pps):
            page = tbl_ref[bb, n_pages - 1 - (pp * pps + u)]
            for h in range(FOX_HEADS):
                cps.append(pltpu.make_async_copy(k_hbm.at[layer, page, :, h, :], kbuf.at[sl, u, h], sem.at[sl]))
                cps.append(pltpu.make_async_copy(v_hbm.at[layer, page, :, h, :], vbuf.at[sl, u, h], sem.at[sl]))
        return cps

    @pl.when(step == 0)
    def _():
        for cp in page_copies(b, p, slot):
            cp.start()

    @pl.when(step + 1 < pl.num_programs(0) * n_steps)
    def _():
        wrap = p + 1 == n_steps
        for cp in page_copies(jnp.where(wrap, b + 1, b), jnp.where(wrap, 0, p + 1), 1 - slot):
            cp.start()

    def update(load_k, load_v, key_bias, masked):
        for g in range(FOX_HEADS // DECODE_HEAD_GROUP):
            heads = range(g * DECODE_HEAD_GROUP, (g + 1) * DECODE_HEAD_GROUP)
            r0 = g * grows
            s_parts = []
            for h in heads:
                qh = q_ref[:, h * hd:(h + 1) * hd]
                s_h = lax.dot_general(qh, load_k(h), (((1,), (1,)), ((), ())), preferred_element_type=F32)
                s_parts.append(s_h * scale + jnp.broadcast_to(key_bias[h:h + 1, :], (QPAD, PAGE)))
            s = jnp.concatenate(s_parts, axis=0) + fq_ref[r0:r0 + grows, :]
            if masked:
                r = lax.broadcasted_iota(jnp.int32, (grows, PAGE), 0) & (QPAD - 1)
                c = lax.broadcasted_iota(jnp.int32, (grows, PAGE), 1)
                s = jnp.where((c <= r) & (c < n_new), s, -jnp.inf)
            m_prev = m_ref[r0:r0 + grows, :]
            m_next = jnp.maximum(m_prev, jnp.max(s, axis=1, keepdims=True))
            pr = jnp.exp(s - m_next)
            alpha = jnp.exp(m_prev - m_next)
            l_ref[r0:r0 + grows, :] = alpha * l_ref[r0:r0 + grows, :] + jnp.sum(pr, axis=1, keepdims=True)
            pb = pr.astype(BF16)
            pv = [jnp.dot(pb[i * QPAD:(i + 1) * QPAD, :], load_v(h), preferred_element_type=F32)
                  for i, h in enumerate(heads)]
            acc_ref[r0:r0 + grows, :] = alpha * acc_ref[r0:r0 + grows, :] + jnp.concatenate(pv, axis=0)
            m_ref[r0:r0 + grows, :] = m_next

    @pl.when(p == 0)
    def _():
        m_ref[...] = jnp.full_like(m_ref, -jnp.inf)
        l_ref[...] = jnp.zeros_like(l_ref)
        acc_ref[...] = jnp.zeros_like(acc_ref)
        carry_ref[...] = jnp.zeros_like(carry_ref)
        ln = ln_ref[...]
        row = lax.broadcasted_iota(jnp.int32, ln.shape, 0)
        cn = jnp.zeros_like(ln)
        for i in range(n_new):
            cn = cn + jnp.where(row >= i, ln[i:i + 1, :], 0.0)
        for h in range(FOX_HEADS):
            fq_ref[h * QPAD:(h + 1) * QPAD, :] = jnp.broadcast_to(cn[:, h:h + 1], (QPAD, LANES))
        cnt = jnp.dot(lnt_ref[...], _tri(PAGE, False), precision=lax.Precision.HIGHEST,
                      preferred_element_type=F32)
        update(lambda h: kn_ref[:, h * hd:(h + 1) * hd],
               lambda h: vn_ref[:, h * hd:(h + 1) * hd],
               -cnt, True)

    for cp in page_copies(b, p, slot):
        cp.wait()
    for u in range(pps):
        lft = lf_refs[u][...]
        suffix = jnp.dot(lft, 1.0 - _tri(PAGE, False), precision=lax.Precision.HIGHEST,
                         preferred_element_type=F32) + carry_ref[...]
        carry_ref[...] = carry_ref[...] + jnp.sum(lft, axis=1, keepdims=True)
        update(lambda h: kbuf[slot, u, h].astype(BF16), lambda h: vbuf[slot, u, h].astype(BF16), suffix, False)

    @pl.when(p == pl.num_programs(1) - 1)
    def _():
        o = acc_ref[...] / l_ref[...]
        for h in range(FOX_HEADS):
            zb = z_ref[:, h * hd:(h + 1) * hd].astype(F32)
            o_ref[:, h * hd:(h + 1) * hd] = (o[h * QPAD:(h + 1) * QPAD, :] * _silu(zb)).astype(o_ref.dtype)


def _fox_decode_attn(table, layer, q16, cache_k, cache_v, logf_t, ln16, ln_t, k_new, v_new, z16, n_new):
    b, n_pages = table.shape
    pps = DECODE_PAGES_PER_STEP
    assert n_pages % pps == 0
    width = FOX_HEADS * FOX_HEAD_DIM
    rows = FOX_HEADS * QPAD
    body = functools.partial(_fox_decode_body, n_new=n_new, layer=layer, n_pages=n_pages)
    per_b = lambda shape: pl.BlockSpec((None,) + shape, lambda i, p, tbl: (i, 0, 0))
    hbm = pl.BlockSpec(memory_space=pl.ANY)
    page_buf = pltpu.VMEM((2, pps, FOX_HEADS, PAGE, FOX_HEAD_DIM), F32)

    def logf_spec(u):
        return pl.BlockSpec((None, FOX_HEADS, PAGE), lambda i, p, tbl: (tbl[i, n_pages - 1 - (p * pps + u)], 0, 0))

    return pl.pallas_call(
        body,
        grid_spec=pltpu.PrefetchScalarGridSpec(
            num_scalar_prefetch=1,
            grid=(b, n_pages // pps),
            in_specs=[
                per_b((QPAD, width)), hbm, hbm,
                *[logf_spec(u) for u in range(pps)],
                per_b((QPAD, FOX_HEADS)), per_b((FOX_HEADS, PAGE)),
                per_b((PAGE, width)), per_b((PAGE, width)),
                per_b((QPAD, width)),
            ],
            out_specs=per_b((QPAD, width)),
            scratch_shapes=[page_buf, page_buf, pltpu.SemaphoreType.DMA((2,)),
                            pltpu.VMEM((rows, LANES), F32), pltpu.VMEM((rows, LANES), F32),
                            pltpu.VMEM((rows, FOX_HEAD_DIM), F32), pltpu.VMEM((rows, LANES), F32),
                            pltpu.VMEM((FOX_HEADS, LANES), F32)],
        ),
        out_shape=jax.ShapeDtypeStruct((b, QPAD, width), BF16),
        compiler_params=_params(("arbitrary", "arbitrary")),
        name="fox_decode_attn",
    )(table, q16, cache_k, cache_v, *([logf_t] * pps), ln16, ln_t, k_new, v_new, z16)


def _conv_body(x_ref, init_ref, w_ref, b_ref, o_ref, carry_ref):
    @pl.when(pl.program_id(2) == 0)
    def _():
        carry_ref[...] = init_ref[...]

    x = x_ref[...].astype(F32)
    o_ref[...] = _causal_conv_silu(x, carry_ref[...], w_ref, b_ref).astype(o_ref.dtype)
    carry_ref[...] = x[x.shape[0] - SUBLANES:, :]


def _conv_silu(x, init, w, bias, batch, t, tt, tc=2048):
    c = x.shape[1]
    nt = t // tt
    return pl.pallas_call(
        _conv_body,
        grid=(batch, c // tc, nt),
        in_specs=[
            pl.BlockSpec((tt, tc), lambda b, j, i: (b * nt + i, j)),
            pl.BlockSpec((None, SUBLANES, tc), lambda b, j, i: (b, 0, j)),
            pl.BlockSpec((SSD_CONV, tc), lambda b, j, i: (0, j)),
            pl.BlockSpec((1, tc), lambda b, j, i: (0, j)),
        ],
        out_specs=pl.BlockSpec((tt, tc), lambda b, j, i: (b * nt + i, j)),
        out_shape=jax.ShapeDtypeStruct(x.shape, BF16),
        scratch_shapes=[pltpu.VMEM((SUBLANES, tc), F32)],
        compiler_params=_params(("arbitrary", "arbitrary", "arbitrary")),
        name="ssd_conv_silu",
    )(x, init, w, bias.reshape(1, c))


SSD_GROUPS_PER_STEP = 2


def _ssd_body(*refs, valid, has_init):
    if has_init:
        (x_ref, b_ref, c_ref, dt_ref, dtt_ref, z_ref, bias_ref, biast_ref, alog_ref, alogt_ref,
         d_ref, nw_ref, s0_ref, y_ref, sout_ref, st_ref, yacc_ref) = refs
    else:
        (x_ref, b_ref, c_ref, dt_ref, dtt_ref, z_ref, bias_ref, biast_ref, alog_ref, alogt_ref,
         d_ref, nw_ref, y_ref, sout_ref, st_ref, yacc_ref) = refs
        s0_ref = None
    ci = pl.program_id(2)
    L = SSD_CHUNK
    gw = SSD_GROUP_WIDTH
    npairs = SSD_HEADS_PER_GROUP // 2
    causal = lax.broadcasted_iota(jnp.int32, (L, L), 0) >= lax.broadcasted_iota(jnp.int32, (L, L), 1)
    first_head = lax.broadcasted_iota(jnp.int32, (L, LANES), 1) < SSD_HEAD_DIM
    first_head_row = lax.broadcasted_iota(jnp.int32, (1, LANES), 1) < SSD_HEAD_DIM

    for u in range(SSD_GROUPS_PER_STEP):
        c0 = u * gw

        @pl.when(ci == 0)
        def _():
            if has_init:
                for k in range(npairs):
                    st_ref[u, :, k * LANES:(k + 1) * LANES] = s0_ref[c0 + k * LANES:c0 + (k + 1) * LANES, :].T
            else:
                st_ref[u] = jnp.zeros(st_ref.shape[1:], F32)

        dt = _softplus(dt_ref[u] + bias_ref[u])
        dtt = _softplus(dtt_ref[u] + biast_ref[u])
        if valid < L:
            dt = jnp.where(lax.broadcasted_iota(jnp.int32, dt.shape, 0) < valid, dt, 0.0)
            dtt = jnp.where(lax.broadcasted_iota(jnp.int32, dtt.shape, 1) < valid, dtt, 0.0)
        acum = jnp.dot(_tri(L, True), dt * (-jnp.exp(alog_ref[u])),
                       precision=lax.Precision.HIGHEST, preferred_element_type=F32) * LOG2E
        acumt = jnp.dot(dtt * (-jnp.exp(alogt_ref[u])), _tri(L, False),
                        precision=lax.Precision.HIGHEST, preferred_element_type=F32) * LOG2E
        last_t = acumt[:, L - 1:L]
        wt = jnp.exp2(last_t - acumt) * dtt
        chunk_decay = jnp.broadcast_to(jnp.exp2(last_t), (SSD_HEADS_PER_GROUP, LANES))
        src_term = acumt - jnp.log2(dtt)

        bm = b_ref[:, u * SSD_STATE:(u + 1) * SSD_STATE]
        cm = c_ref[:, u * SSD_STATE:(u + 1) * SSD_STATE]
        cb = lax.dot_general(cm, bm, (((1,), (1,)), ((), ())), preferred_element_type=F32)
        cm32 = cm.astype(F32)
        bmt = bm.astype(F32).T

        for k in range(npairs):
            cols = slice(c0 + k * LANES, c0 + (k + 1) * LANES)
            xp = x_ref[:, cols]
            sp = st_ref[u, :, k * LANES:(k + 1) * LANES]
            rhs = jnp.concatenate([xp, sp.astype(BF16)], axis=0)
            ys, ups = [], []
            for r in (2 * k, 2 * k + 1):
                a_col = jnp.broadcast_to(acum[:, r:r + 1], (L, L))
                m = cb * jnp.exp2(jnp.where(causal, a_col - src_term[r:r + 1, :], -jnp.inf))
                ec = cm32 * jnp.exp2(a_col)
                lhs = jnp.concatenate([m, ec], axis=1).astype(BF16)
                ys.append(jnp.dot(lhs, rhs, preferred_element_type=F32))
                bw = (bmt * wt[r:r + 1, :]).astype(BF16)
                ups.append(jnp.dot(bw, xp, preferred_element_type=F32))
            yacc_ref[:, cols] = jnp.where(first_head, ys[0], ys[1])
            dec_row = jnp.where(first_head_row, chunk_decay[2 * k:2 * k + 1, :],
                                chunk_decay[2 * k + 1:2 * k + 2, :])
            st_ref[u, :, k * LANES:(k + 1) * LANES] = sp * dec_row + jnp.where(first_head, ups[0], ups[1])

        gcols = slice(c0, c0 + gw)
        y = yacc_ref[:, gcols] + d_ref[u] * x_ref[:, gcols].astype(F32)
        y = y * _silu(z_ref[:, gcols].astype(F32))
        y = y * lax.rsqrt(jnp.mean(y * y, axis=-1, keepdims=True) + EPS) * nw_ref[u]
        y_ref[:, gcols] = y.astype(y_ref.dtype)

        @pl.when(ci == pl.num_programs(2) - 1)
        def _():
            for k in range(npairs):
                sout_ref[c0 + k * LANES:c0 + (k + 1) * LANES, :] = st_ref[u, :, k * LANES:(k + 1) * LANES].T


def _ssd_scan(act, dt_g, dtt_g, z, dt_bias, a_log, d_skip, norm_w, state0, batch, t, valid):
    L = SSD_CHUNK
    nc = t // L
    gps = SSD_GROUPS_PER_STEP
    gw = SSD_GROUP_WIDTH
    hpg = SSD_HEADS_PER_GROUP
    g = SSD_GROUPS
    inner = g * gw
    b_blk0 = inner // (gps * SSD_STATE)
    c_blk0 = b_blk0 + g // gps
    has_init = state0 is not None
    body = functools.partial(_ssd_body, valid=valid, has_init=has_init)
    grp = lambda shape: pl.BlockSpec((gps,) + shape, lambda b, gi, c: (gi, 0, 0))
    in_specs = [
        pl.BlockSpec((L, gps * gw), lambda b, gi, c: (b * nc + c, gi)),
        pl.BlockSpec((L, gps * SSD_STATE), lambda b, gi, c: (b * nc + c, b_blk0 + gi)),
        pl.BlockSpec((L, gps * SSD_STATE), lambda b, gi, c: (b * nc + c, c_blk0 + gi)),
        pl.BlockSpec((gps, L, hpg), lambda b, gi, c: (gi, b * nc + c, 0)),
        pl.BlockSpec((gps, hpg, L), lambda b, gi, c: (gi, 0, b * nc + c)),
        pl.BlockSpec((L, gps * gw), lambda b, gi, c: (b * nc + c, gi)),
        grp((1, hpg)), grp((hpg, 1)), grp((1, hpg)), grp((hpg, 1)),
        grp((1, gw)), grp((1, gw)),
    ]
    args = [act, act, act, dt_g, dtt_g, z,
            dt_bias.reshape(g, 1, hpg), dt_bias.reshape(g, hpg, 1),
            a_log.reshape(g, 1, hpg), a_log.reshape(g, hpg, 1),
            jnp.repeat(d_skip, SSD_HEAD_DIM).reshape(g, 1, gw), norm_w.reshape(g, 1, gw)]
    state_spec = pl.BlockSpec((None, gps * gw, SSD_STATE), lambda b, gi, c: (b, gi, 0))
    if has_init:
        in_specs.append(state_spec)
        args.append(state0.reshape(batch, g * gw, SSD_STATE))
    y, s_out = pl.pallas_call(
        body,
        grid=(batch, g // gps, nc),
        in_specs=in_specs,
        out_specs=[pl.BlockSpec((L, gps * gw), lambda b, gi, c: (b * nc + c, gi)), state_spec],
        out_shape=[jax.ShapeDtypeStruct((batch * t, inner), BF16),
                   jax.ShapeDtypeStruct((batch, g * gw, SSD_STATE), F32)],
        scratch_shapes=[pltpu.VMEM((gps, SSD_STATE, gw), F32), pltpu.VMEM((L, gps * gw), F32)],
        compiler_params=_params(("arbitrary", "arbitrary", "arbitrary")),
        name="ssd_scan",
    )(*args)
    return y, s_out.reshape(batch, g * hpg, SSD_HEAD_DIM, SSD_STATE)


def _mod_parts(mod_l, nb_prompt, nb_sample, rows_per_sample, d):
    def prompt(a):
        return a[:nb_prompt].reshape(nb_prompt, 1, d)

    def sample(a):
        return jnp.repeat(a[nb_prompt:nb_prompt + nb_sample], rows_per_sample, axis=0)

    parts = [mod_l[:, i * d:(i + 1) * d] for i in range(3)]
    return [prompt(a) for a in parts], [sample(a) for a in parts]


def _pad_rows(a, batch, t, t_pad):
    c = a.shape[-1]
    a = a.reshape(batch, t, c)
    return jnp.pad(a, ((0, 0), (0, t_pad - t), (0, 0))).reshape(batch * t_pad, c)


def kernel(x_prompt, x_sample, c_prompt, c_sample, cache_k, cache_v, cache_logf, state_ssm, state_conv,
           page_table, norm_w, ada_w, ada_b, fox_w_in, fox_b_f, fox_w_out, ssd_w_in, ssd_conv_w, ssd_conv_b,
           ssd_dt_bias, ssd_a_log, ssd_d, ssd_norm_w, ssd_w_out, final_norm_w):
    bp, t, d = x_prompt.shape
    bs, ts, _ = x_sample.shape
    depth = norm_w.shape[0]
    n_fox = fox_w_in.shape[0]
    mp, ms = bp * t, bs * ts
    width = FOX_HEADS * FOX_HEAD_DIM
    inner = SSD_GROUPS * SSD_GROUP_WIDTH
    conv_dim = inner + 2 * SSD_GROUPS * SSD_STATE
    n_heads = SSD_GROUPS * SSD_HEADS_PER_GROUP

    xp = x_prompt.reshape(mp, d)
    xs = x_sample.reshape(ms, d)

    c_all = jnp.concatenate([c_prompt, c_sample], axis=0)
    c_all = jnp.pad(c_all, ((0, (-c_all.shape[0]) % SUBLANES), (0, 0)))
    mod = _ada_mod(c_all, ada_w, ada_b)

    fox_w_in16 = fox_w_in.astype(BF16)
    fox_w_out16 = fox_w_out.astype(BF16)
    ssd_w_in16 = ssd_w_in.astype(BF16)
    ssd_w_out16 = ssd_w_out.astype(BF16)
    fox_w_f16 = jnp.pad(fox_w_in[:, :, 4 * width:], ((0, 0), (0, 0), (0, LANES - FOX_HEADS))).astype(BF16)

    kp_all = jnp.zeros((n_fox, mp, width), F32)
    vp_all = jnp.zeros((n_fox, mp, width), F32)
    outs ={name: [] for name in ("fp", "ks", "vs", "fs", "sp", "cp", "ss", "cs")}
    for i in range(depth):
        j = i // 2
        (sh_p, sc_p, g_p), (sh_s, sc_s, g_s) = _mod_parts(mod[i], bp, bs, ts, d)
        hp = _norm_mod(xp, norm_w[i], sc_p, sh_p, t, 256)
        hs = _norm_mod(xs, norm_w[i], sc_s, sh_s, ts, ms)
        if i % 2 == 0:
            b_f = jnp.pad(fox_b_f[j], (0, LANES - FOX_HEADS)).reshape(1, LANES)

            def logf_of(h):
                return _mm_logsig(h, fox_w_f16[j:j + 1], b_f, "fox_logf")[:, :FOX_HEADS]

            (q,) = _mm_plain(hp, fox_w_in16, j, 0, width, [BF16], "fox_q")
            kp_all, k16 = _mm_kv(hp, fox_w_in16, j, width, width, kp_all, "fox_k")
            vp_all, v16 = _mm_kv(hp, fox_w_in16, j, 2 * width, width, vp_all, "fox_v")
            (z,) = _mm_plain(hp, fox_w_in16, j, 3 * width, width, [BF16], "fox_z")
            logf = logf_of(hp)
            f_cum = _time_cumsum(logf.reshape(bp, t, FOX_HEADS))
            og = _fox_prompt_attn(q, k16, v16, z, jnp.swapaxes(f_cum, 1, 2), bp, t, tq=min(512, t), tk=min(512, t))
            xp = _mm_resid(og, fox_w_out16, j, xp, g_p, t, "fox_out")
            outs["fp"].append(logf.reshape(bp, t, FOX_HEADS))

            (q,) = _mm_plain(hs, fox_w_in16, j, 0, width, [BF16], "fox_q_s")
            k32, k16 = _mm_plain(hs, fox_w_in16, j, width, width, [F32, BF16], "fox_k_s")
            v32, v16 = _mm_plain(hs, fox_w_in16, j, 2 * width, width, [F32, BF16], "fox_v_s")
            (z,) = _mm_plain(hs, fox_w_in16, j, 3 * width, width, [BF16], "fox_z_s")
            logf = logf_of(hs)

            def pad_q(a, rows):
                return jnp.pad(a.reshape(bs, ts, -1), ((0, 0), (0, rows - ts), (0, 0)))

            ln16 = pad_q(logf, QPAD)
            ln_t = jnp.swapaxes(pad_q(logf, PAGE), 1, 2)
            og = _fox_decode_attn(page_table, j, pad_q(q, QPAD), cache_k, cache_v,
                                  jnp.swapaxes(cache_logf[j], 1, 2), ln16, ln_t,
                                  pad_q(k16, PAGE), pad_q(v16, PAGE), pad_q(z, QPAD), ts)
            og = og[:, :ts].reshape(ms, width)
            xs = _mm_resid(og, fox_w_out16, j, xs, g_s, ts, "fox_out_s")
            outs["ks"].append(k32.reshape(bs, ts, FOX_HEADS, FOX_HEAD_DIM))
            outs["vs"].append(v32.reshape(bs, ts, FOX_HEADS, FOX_HEAD_DIM))
            outs["fs"].append(logf.reshape(bs, ts, FOX_HEADS))
        else:
            def mixer(h, x_res, gate, batch, tt_, conv_prev, state0, suffix):
                m = batch * tt_
                (zz,) = _mm_plain(h, ssd_w_in16, j, 0, inner, [BF16], "ssd_z" + suffix)
                (dt_raw,) = _mm_plain(h, ssd_w_in16, j, inner + conv_dim, n_heads, [F32], "ssd_dt" + suffix)
                t_pad = -(-tt_ // SSD_CHUNK) * SSD_CHUNK
                if conv_prev is None:
                    assert t_pad == tt_
                    act, tail = _mm_conv(h, ssd_w_in16, j, inner, conv_dim, tt_, ssd_conv_w[j], ssd_conv_b[j],
                                         "ssd_xbc_conv" + suffix)
                    new_conv = tail[:, SUBLANES - (SSD_CONV - 1):]
                else:
                    (xbc,) = _mm_plain(h, ssd_w_in16, j, inner, conv_dim, [F32], "ssd_xbc" + suffix)
                    init = jnp.pad(conv_prev, ((0, 0), (SUBLANES - (SSD_CONV - 1), 0), (0, 0)))
                    full = jnp.concatenate([conv_prev, xbc.reshape(batch, tt_, conv_dim)], axis=1)
                    new_conv = full[:, tt_:]
                    if t_pad != tt_:
                        xbc = _pad_rows(xbc, batch, tt_, t_pad)
                    act = _conv_silu(xbc, init, ssd_conv_w[j], ssd_conv_b[j], batch, t_pad, min(512, t_pad))
                if t_pad != tt_:
                    zz, dt_raw = (_pad_rows(a, batch, tt_, t_pad) for a in (zz, dt_raw))
                dt_g = dt_raw.reshape(batch * t_pad, SSD_GROUPS, SSD_HEADS_PER_GROUP).transpose(1, 0, 2)
                dtt_g = dt_g.transpose(0, 2, 1)
                y, new_state = _ssd_scan(act, dt_g, dtt_g, zz, ssd_dt_bias[j], ssd_a_log[j], ssd_d[j],
                                         ssd_norm_w[j], state0, batch, t_pad, min(tt_, SSD_CHUNK))
                if t_pad != tt_:
                    y = y.reshape(batch, t_pad, inner)[:, :tt_].reshape(m, inner)
                x_new = _mm_resid(y, ssd_w_out16, j, x_res, gate, tt_, "ssd_out" + suffix)
                return x_new, new_conv, new_state

            xp, cvp, smp = mixer(hp, xp, g_p, bp, t, None, None, "")
            xs, cvs, sms = mixer(hs, xs, g_s, bs, ts, state_conv[j], state_ssm[j], "_s")
            outs["sp"].append(smp)
            outs["cp"].append(cvp)
            outs["ss"].append(sms)
            outs["cs"].append(cvs)

    y_prompt = _final_norm(xp, final_norm_w, 256).reshape(bp, t, d)
    y_sample = _final_norm(xs, final_norm_w, ms).reshape(bs, ts, d)
    kv_shape = (n_fox, bp, t, FOX_HEADS, FOX_HEAD_DIM)
    return (y_prompt, y_sample,
            kp_all.reshape(kv_shape), vp_all.reshape(kv_shape), jnp.stack(outs["fp"]),
            jnp.stack(outs["ks"]), jnp.stack(outs["vs"]), jnp.stack(outs["fs"]),
            jnp.stack(outs["sp"]), jnp.stack(outs["cp"]), jnp.stack(outs["ss"]), jnp.stack(outs["cs"]))
```

```python
import functools
import math

import jax
import jax.numpy as jnp
from jax import lax
from jax.experimental import pallas as pl
from jax.experimental.pallas import tpu as pltpu

F32 = jnp.float32
BF16 = jnp.bfloat16

EPS = 1e-6
LOG2E = math.log2(math.e)
LANES = 128
SUBLANES = 8
PAGE = 128
FOX_HEADS = 32
FOX_HEAD_DIM = 128
SSD_GROUPS = 8
SSD_HEAD_DIM = 64
SSD_STATE = 128
SSD_HEADS_PER_GROUP = 16
SSD_GROUP_WIDTH = SSD_HEADS_PER_GROUP * SSD_HEAD_DIM
SSD_CHUNK = 128
SSD_CONV = 4
V7X_VMEM_BYTES = 64 * 1024 * 1024
VMEM_LIMIT = V7X_VMEM_BYTES * 7 // 8
MM_TILE_BUDGET = 48 * 1024 * 1024


def _params(semantics):
    return pltpu.CompilerParams(dimension_semantics=semantics, vmem_limit_bytes=VMEM_LIMIT)


def _silu(x):
    return x * (1.0 / (1.0 + jnp.exp2(x * -LOG2E)))


def _softplus(x):
    return jnp.maximum(x, 0.0) + jnp.log1p(jnp.exp(-jnp.abs(x)))


def _tri(n, lower):
    r = lax.broadcasted_iota(jnp.int32, (n, n), 0)
    c = lax.broadcasted_iota(jnp.int32, (n, n), 1)
    return ((r >= c) if lower else (r <= c)).astype(F32)


def _lane_tile(x, reps):
    return x if reps == 1 else jnp.concatenate([x] * reps, axis=1)


def _ada_body(c_ref, w_ref, b_ref, o_ref):
    a = _silu(c_ref[...]).astype(BF16)
    o_ref[...] = jnp.dot(a, w_ref[...].astype(BF16), preferred_element_type=F32) + b_ref[...]


def _ada_mod(c_all, ada_w, ada_b, tn=512):
    depth, d, n = ada_w.shape
    rows = c_all.shape[0]
    return pl.pallas_call(
        _ada_body,
        grid=(depth, n // tn),
        in_specs=[
            pl.BlockSpec((rows, d), lambda l, j: (0, 0)),
            pl.BlockSpec((None, d, tn), lambda l, j: (l, 0, j)),
            pl.BlockSpec((None, 1, tn), lambda l, j: (l, 0, j)),
        ],
        out_specs=pl.BlockSpec((None, rows, tn), lambda l, j: (l, 0, j)),
        out_shape=jax.ShapeDtypeStruct((depth, rows, n), F32),
        compiler_params=_params(("arbitrary", "arbitrary")),
        name="ada_mod",
    )(c_all, ada_w, ada_b.reshape(depth, 1, n))


def _norm_mod_body(x_ref, g_ref, sc_ref, sh_ref, o_ref):
    x = x_ref[...]
    y = x * lax.rsqrt(jnp.mean(x * x, axis=-1, keepdims=True) + EPS) * g_ref[...]
    o_ref[...] = (y * (1.0 + sc_ref[...]) + sh_ref[...]).astype(o_ref.dtype)


def _norm_body(x_ref, g_ref, o_ref):
    x = x_ref[...]
    o_ref[...] = (x * lax.rsqrt(jnp.mean(x * x, axis=-1, keepdims=True) + EPS) * g_ref[...]).astype(o_ref.dtype)


def _row_or_batch_spec(arr, tr, tn, rows_per_batch):
    if arr.ndim == 3:
        tiles_per_batch = rows_per_batch // tr
        return pl.BlockSpec((None, 1, tn), lambda i, j: (i // tiles_per_batch, 0, j))
    return pl.BlockSpec((tr, tn), lambda i, j: (i, j))


def _norm_mod(x, g, scale, shift, rows_per_batch, tr):
    m, d = x.shape
    return pl.pallas_call(
        _norm_mod_body,
        grid=(m // tr, 1),
        in_specs=[
            pl.BlockSpec((tr, d), lambda i, j: (i, 0)),
            pl.BlockSpec((1, d), lambda i, j: (0, 0)),
            _row_or_batch_spec(scale, tr, d, rows_per_batch),
            _row_or_batch_spec(shift, tr, d, rows_per_batch),
        ],
        out_specs=pl.BlockSpec((tr, d), lambda i, j: (i, 0)),
        out_shape=jax.ShapeDtypeStruct((m, d), BF16),
        compiler_params=_params(("arbitrary", "arbitrary")),
        name="norm_mod",
    )(x, g.reshape(1, d), scale, shift)


def _final_norm(x, g, tr):
    m, d = x.shape
    return pl.pallas_call(
        _norm_body,
        grid=(m // tr,),
        in_specs=[pl.BlockSpec((tr, d), lambda i: (i, 0)), pl.BlockSpec((1, d), lambda i: (0, 0))],
        out_specs=pl.BlockSpec((tr, d), lambda i: (i, 0)),
        out_shape=jax.ShapeDtypeStruct((m, d), F32),
        compiler_params=_params(("arbitrary",)),
        name="final_norm",
    )(x, g.reshape(1, d))


def _mm_tiles(m, k, n, out_bytes_per_elem):
    for tm, tn in ((1024, 1024), (1024, 512), (512, 512), (512, 256), (256, 256)):
        tm_, tn_ = min(tm, m), min(tn, n)
        if m % tm_ or n % tn_:
            continue
        need = 2 * (tm_ * k * 2 + k * tn_ * 2 + tm_ * tn_ * out_bytes_per_elem)
        if need <= MM_TILE_BUDGET:
            return tm_, tn_
    raise ValueError(f"no matmul tiling for {(m, k, n)}")


def _mm_plain_body(a_ref, w_ref, *o_refs):
    acc = jnp.dot(a_ref[...], w_ref[...], preferred_element_type=F32)
    for o_ref in o_refs:
        o_ref[...] = acc.astype(o_ref.dtype)


def _causal_conv_silu(x, prev, w_ref, b_ref):
    row = lax.broadcasted_iota(jnp.int32, prev.shape, 0)
    acc = b_ref[...] + x * w_ref[SSD_CONV - 1:SSD_CONV, :]
    for shift in range(1, SSD_CONV):
        rolled = pltpu.roll(x, shift, 0)
        top = jnp.where(row < shift, pltpu.roll(prev, shift, 0), rolled[0:SUBLANES, :])
        shifted = jnp.concatenate([top, rolled[SUBLANES:, :]], axis=0)
        acc = acc + shifted * w_ref[SSD_CONV - 1 - shift:SSD_CONV - shift, :]
    return _silu(acc)


def _mm_conv_body(a_ref, w_ref, cw_ref, cb_ref, o_ref, tail_ref, carry_ref, *, tiles_per_batch):
    i = pl.program_id(0)
    j = pl.program_id(1)
    acc = jnp.dot(a_ref[...], w_ref[...], preferred_element_type=F32)
    last = acc[acc.shape[0] - SUBLANES:, :]
    tail_ref[...] = last

    @pl.when(i % tiles_per_batch == 0)
    def _():
        carry_ref[j] = jnp.zeros_like(last)

    o_ref[...] = _causal_conv_silu(acc, carry_ref[j], cw_ref, cb_ref).astype(o_ref.dtype)
    carry_ref[j] = last


def _mm_kv_body(a_ref, w_ref, stacked_ref, o32_ref, o16_ref):
    del stacked_ref
    acc = jnp.dot(a_ref[...], w_ref[...], preferred_element_type=F32)
    o32_ref[...] = acc
    o16_ref[...] = acc.astype(o16_ref.dtype)


def _mm_logsig_body(a_ref, w_ref, b_ref, o_ref):
    acc = jnp.dot(a_ref[...], w_ref[...], preferred_element_type=F32) + b_ref[...]
    o_ref[...] = -_softplus(-acc)


def _mm_resid_body(a_ref, w_ref, x_ref, g_ref, o_ref):
    acc = jnp.dot(a_ref[...], w_ref[...], preferred_element_type=F32)
    o_ref[...] = x_ref[...] + g_ref[...] * acc


def _mm_specs(a, layer, col0, tm, tn):
    k = a.shape[1]
    off = col0 // tn
    return [pl.BlockSpec((tm, k), lambda i, j: (i, 0)),
            pl.BlockSpec((None, k, tn), lambda i, j: (layer, 0, j + off))]


def _mm_plain(a, w, layer, col0, n, out_dtypes, name):
    m, k = a.shape
    tm, tn = _mm_tiles(m, k, n, sum(jnp.dtype(dt).itemsize for dt in out_dtypes))
    return pl.pallas_call(
        _mm_plain_body,
        grid=(m // tm, n // tn),
        in_specs=_mm_specs(a, layer, col0, tm, tn),
        out_specs=[pl.BlockSpec((tm, tn), lambda i, j: (i, j)) for _ in out_dtypes],
        out_shape=[jax.ShapeDtypeStruct((m, n), dt) for dt in out_dtypes],
        compiler_params=_params(("arbitrary", "arbitrary")),
        name=name,
    )(a, w)


def _mm_conv(a, w, layer, col0, n, rows_per_batch, conv_w, conv_b, name):
    m, k = a.shape
    tm, tn = _mm_tiles(rows_per_batch, k, n, 2)
    tiles_per_batch = rows_per_batch // tm
    body = functools.partial(_mm_conv_body, tiles_per_batch=tiles_per_batch)
    act, tails = pl.pallas_call(
        body,
        grid=(m // tm, n // tn),
        in_specs=_mm_specs(a, layer, col0, tm, tn) + [
            pl.BlockSpec((SSD_CONV, tn), lambda i, j: (0, j)),
            pl.BlockSpec((1, tn), lambda i, j: (0, j)),
        ],
        out_specs=[pl.BlockSpec((tm, tn), lambda i, j: (i, j)),
                   pl.BlockSpec((None, SUBLANES, tn), lambda i, j: (i, 0, j))],
        out_shape=[jax.ShapeDtypeStruct((m, n), BF16),
                   jax.ShapeDtypeStruct((m // tm, SUBLANES, n), F32)],
        scratch_shapes=[pltpu.VMEM((n // tn, SUBLANES, tn), F32)],
        compiler_params=_params(("arbitrary", "arbitrary")),
        name=name,
    )(a, w, conv_w, conv_b.reshape(1, n))
    return act, tails[tiles_per_batch - 1::tiles_per_batch]


def _mm_kv(a, w, layer, col0, n, stacked, name):
    m, k = a.shape
    tm, tn = _mm_tiles(m, k, n, 6)
    return pl.pallas_call(
        _mm_kv_body,
        grid=(m // tm, n // tn),
        in_specs=_mm_specs(a, layer, col0, tm, tn) + [pl.BlockSpec(memory_space=pl.ANY)],
        out_specs=[pl.BlockSpec((None, tm, tn), lambda i, j: (layer, i, j)),
                   pl.BlockSpec((tm, tn), lambda i, j: (i, j))],
        out_shape=[jax.ShapeDtypeStruct(stacked.shape, F32), jax.ShapeDtypeStruct((m, n), BF16)],
        input_output_aliases={2: 0},
        compiler_params=_params(("arbitrary", "arbitrary")),
        name=name,
    )(a, w, stacked)


def _mm_logsig(a, w, bias, name):
    m, k = a.shape
    n = w.shape[-1]
    tm = min(m, 1024)
    return pl.pallas_call(
        _mm_logsig_body,
        grid=(m // tm, 1),
        in_specs=_mm_specs(a, 0, 0, tm, n) + [pl.BlockSpec((1, n), lambda i, j: (0, 0))],
        out_specs=pl.BlockSpec((tm, n), lambda i, j: (i, 0)),
        out_shape=jax.ShapeDtypeStruct((m, n), F32),
        compiler_params=_params(("arbitrary", "arbitrary")),
        name=name,
    )(a, w, bias)


def _mm_resid(a, w, layer, x, gate, rows_per_batch, name):
    m, n = x.shape
    k = a.shape[1]
    tm, tn = _mm_tiles(min(m, rows_per_batch) if gate.ndim == 3 else m, k, n, 8)
    return pl.pallas_call(
        _mm_resid_body,
        grid=(m // tm, n // tn),
        in_specs=_mm_specs(a, layer, 0, tm, tn) + [
            pl.BlockSpec((tm, tn), lambda i, j: (i, j)),
            _row_or_batch_spec(gate, tm, tn, rows_per_batch),
        ],
        out_specs=pl.BlockSpec((tm, tn), lambda i, j: (i, j)),
        out_shape=jax.ShapeDtypeStruct((m, n), F32),
        compiler_params=_params(("arbitrary", "arbitrary")),
        name=name,
    )(a, w, x, gate)


def _cumsum_body(x_ref, f_ref, carry_ref):
    @pl.when(pl.program_id(1) == 0)
    def _():
        carry_ref[...] = jnp.zeros_like(carry_ref)

    f = jnp.dot(_tri(PAGE, True), x_ref[...], precision=lax.Precision.HIGHEST,
                preferred_element_type=F32) + carry_ref[0:1, :]
    f_ref[...] = f
    carry_ref[...] = jnp.broadcast_to(f[PAGE - 1:PAGE, :], carry_ref.shape)


def _time_cumsum(x):
    b, t, h = x.shape
    spec = pl.BlockSpec((None, PAGE, h), lambda i, p: (i, p, 0))
    return pl.pallas_call(
        _cumsum_body,
        grid=(b, t // PAGE),
        in_specs=[spec],
        out_specs=spec,
        out_shape=jax.ShapeDtypeStruct(x.shape, F32),
        scratch_shapes=[pltpu.VMEM((SUBLANES, h), F32)],
        compiler_params=_params(("arbitrary", "arbitrary")),
        name="logf_cumsum",
    )(x)


def _fox_prompt_body(q_ref, k_ref, v_ref, z_ref, f_ref, o_ref, m_ref, l_ref, acc_ref, fq_ref, *, tq, tk):
    t = q_ref.shape[0]
    qk_scale = FOX_HEAD_DIM ** -0.5 * LOG2E
    nrep = tk // LANES
    for qi in range(t // tq):
        q0 = qi * tq
        q = q_ref[q0:q0 + tq, :]
        fq_ref[...] = jnp.broadcast_to(f_ref[qi] * LOG2E, (LANES, tq)).T
        m_ref[...] = jnp.full_like(m_ref, -jnp.inf)
        l_ref[...] = jnp.zeros_like(l_ref)
        acc_ref[...] = jnp.zeros_like(acc_ref)

        def kv_step(j, k0, masked):
            kb = k_ref[pl.ds(k0, tk), :]
            vb = v_ref[pl.ds(k0, tk), :]
            s = lax.dot_general(q, kb, (((1,), (1,)), ((), ())), preferred_element_type=F32) * qk_scale
            s = s - f_ref[j] * LOG2E
            if masked:
                r = lax.broadcasted_iota(jnp.int32, (tq, tk), 0)
                c = lax.broadcasted_iota(jnp.int32, (tq, tk), 1)
                s = jnp.where(c <= r, s, -jnp.inf)
            fq = fq_ref[...]
            m_prev = m_ref[...]
            m_next = jnp.maximum(m_prev, jnp.max(s, axis=1, keepdims=True) + fq)
            p = jnp.exp2(s + _lane_tile(fq - m_next, nrep))
            alpha = jnp.exp2(m_prev - m_next)
            l_ref[...] = alpha * l_ref[...] + jnp.sum(p, axis=1, keepdims=True)
            acc_ref[...] = alpha * acc_ref[...] + jnp.dot(p.astype(BF16), vb, preferred_element_type=F32)
            m_ref[...] = m_next

        for j in range(qi):
            kv_step(j, j * tk, False)
        kv_step(qi, q0, True)
        o = acc_ref[...] / l_ref[...]
        zb = z_ref[q0:q0 + tq, :].astype(F32)
        o_ref[q0:q0 + tq, :] = (o * _silu(zb)).astype(o_ref.dtype)


def _fox_prompt_attn(q, k, v, z, f_t, batch, t, tq=512, tk=512):
    assert tq == tk and t % tq == 0
    hd = FOX_HEAD_DIM
    nk = t // tk
    f_t = f_t.reshape(batch, FOX_HEADS, nk, 1, tk)
    body = functools.partial(_fox_prompt_body, tq=tq, tk=tk)
    row_spec = pl.BlockSpec((t, hd), lambda b, h: (b, h))
    return pl.pallas_call(
        body,
        grid=(batch, FOX_HEADS),
        in_specs=[row_spec, row_spec, row_spec, row_spec,
                  pl.BlockSpec((None, None, nk, 1, tk), lambda b, h: (b, h, 0, 0, 0))],
        out_specs=row_spec,
        out_shape=jax.ShapeDtypeStruct(q.shape, BF16),
        scratch_shapes=[pltpu.VMEM((tq, LANES), F32), pltpu.VMEM((tq, LANES), F32),
                        pltpu.VMEM((tq, hd), F32), pltpu.VMEM((tq, LANES), F32)],
        compiler_params=_params(("arbitrary", "arbitrary")),
        name="fox_prompt_attn",
    )(q, k, v, z, f_t)


QPAD = 16
DECODE_HEAD_GROUP = FOX_HEADS


DECODE_PAGES_PER_STEP = 4


def _fox_decode_body(tbl_ref, q_ref, k_hbm, v_hbm, *refs, n_new, layer, n_pages):
    pps = DECODE_PAGES_PER_STEP
    lf_refs = refs[:pps]
    (ln_ref, lnt_ref, kn_ref, vn_ref, z_ref, o_ref,
     kbuf, vbuf, sem, m_ref, l_ref, acc_ref, fq_ref, carry_ref) = refs[pps:]
    b = pl.program_id(0)
    p = pl.program_id(1)
    n_steps = pl.num_programs(1)
    step = b * n_steps + p
    slot = step % 2
    scale = FOX_HEAD_DIM ** -0.5
    hd = FOX_HEAD_DIM
    grows = DECODE_HEAD_GROUP * QPAD

    def page_copies(bb, pp, sl):
        cps = []
        for u in range(pps):
            page = tbl_ref[bb, n_pages - 1 - (pp * pps + u)]
            for h in range(FOX_HEADS):
                cps.append(pltpu.make_async_copy(k_hbm.at[layer, page, :, h, :], kbuf.at[sl, u, h], sem.at[sl]))
                cps.append(pltpu.make_async_copy(v_hbm.at[layer, page, :, h, :], vbuf.at[sl, u, h], sem.at[sl]))
        return cps

    @pl.when(step == 0)
    def _():
        for cp in page_copies(b, p, slot):
            cp.start()

    @pl.when(step + 1 < pl.num_programs(0) * n_steps)
    def _():
        wrap = p + 1 == n_steps
        for cp in page_copies(jnp.where(wrap, b + 1, b), jnp.where(wrap, 0, p + 1), 1 - slot):
            cp.start()

    def update(load_k, load_v, key_bias, masked):
        for g in range(FOX_HEADS // DECODE_HEAD_GROUP):
            heads = range(g * DECODE_HEAD_GROUP, (g + 1) * DECODE_HEAD_GROUP)
            r0 = g * grows
            s_parts = []
            for h in heads:
                qh = q_ref[:, h * hd:(h + 1) * hd]
                s_h = lax.dot_general(qh, load_k(h), (((1,), (1,)), ((), ())), preferred_element_type=F32)
                s_parts.append(s_h * scale + jnp.broadcast_to(key_bias[h:h + 1, :], (QPAD, PAGE)))
            s = jnp.concatenate(s_parts, axis=0) + fq_ref[r0:r0 + grows, :]
            if masked:
                r = lax.broadcasted_iota(jnp.int32, (grows, PAGE), 0) & (QPAD - 1)
                c = lax.broadcasted_iota(jnp.int32, (grows, PAGE), 1)
                s = jnp.where((c <= r) & (c < n_new), s, -jnp.inf)
            m_prev = m_ref[r0:r0 + grows, :]
            m_next = jnp.maximum(m_prev, jnp.max(s, axis=1, keepdims=True))
            pr = jnp.exp(s - m_next)
            alpha = jnp.exp(m_prev - m_next)
            l_ref[r0:r0 + grows, :] = alpha * l_ref[r0:r0 + grows, :] + jnp.sum(pr, axis=1, keepdims=True)
            pb = pr.astype(BF16)
            pv = [jnp.dot(pb[i * QPAD:(i + 1) * QPAD, :], load_v(h), preferred_element_type=F32)
                  for i, h in enumerate(heads)]
            acc_ref[r0:r0 + grows, :] = alpha * acc_ref[r0:r0 + grows, :] + jnp.concatenate(pv, axis=0)
            m_ref[r0:r0 + grows, :] = m_next

    @pl.when(p == 0)
    def _():
        m_ref[...] = jnp.full_like(m_ref, -jnp.inf)
        l_ref[...] = jnp.zeros_like(l_ref)
        acc_ref[...] = jnp.zeros_like(acc_ref)
        carry_ref[...] = jnp.zeros_like(carry_ref)
        ln = ln_ref[...]
        row = lax.broadcasted_iota(jnp.int32, ln.shape, 0)
        cn = jnp.zeros_like(ln)
        for i in range(n_new):
            cn = cn + jnp.where(row >= i, ln[i:i + 1, :], 0.0)
        for h in range(FOX_HEADS):
            fq_ref[h * QPAD:(h + 1) * QPAD, :] = jnp.broadcast_to(cn[:, h:h + 1], (QPAD, LANES))
        cnt = jnp.dot(lnt_ref[...], _tri(PAGE, False), precision=lax.Precision.HIGHEST,
                      preferred_element_type=F32)
        update(lambda h: kn_ref[:, h * hd:(h + 1) * hd],
               lambda h: vn_ref[:, h * hd:(h + 1) * hd],
               -cnt, True)

    for cp in page_copies(b, p, slot):
        cp.wait()
    for u in range(pps):
        lft = lf_refs[u][...]
        suffix = jnp.dot(lft, 1.0 - _tri(PAGE, False), precision=lax.Precision.HIGHEST,
                         preferred_element_type=F32) + carry_ref[...]
        carry_ref[...] = carry_ref[...] + jnp.sum(lft, axis=1, keepdims=True)
        update(lambda h: kbuf[slot, u, h].astype(BF16), lambda h: vbuf[slot, u, h].astype(BF16), suffix, False)

    @pl.when(p == pl.num_programs(1) - 1)
    def _():
        o = acc_ref[...] / l_ref[...]
        for h in range(FOX_HEADS):
            zb = z_ref[:, h * hd:(h + 1) * hd].astype(F32)
            o_ref[:, h * hd:(h + 1) * hd] = (o[h * QPAD:(h + 1) * QPAD, :] * _silu(zb)).astype(o_ref.dtype)


def _fox_decode_attn(table, layer, q16, cache_k, cache_v, logf_t, ln16, ln_t, k_new, v_new, z16, n_new):
    b, n_pages = table.shape
    pps = DECODE_PAGES_PER_STEP
    assert n_pages % pps == 0
    width = FOX_HEADS * FOX_HEAD_DIM
    rows = FOX_HEADS * QPAD
    body = functools.partial(_fox_decode_body, n_new=n_new, layer=layer, n_pages=n_pages)
    per_b = lambda shape: pl.BlockSpec((None,) + shape, lambda i, p, tbl: (i, 0, 0))
    hbm = pl.BlockSpec(memory_space=pl.ANY)
    page_buf = pltpu.VMEM((2, pps, FOX_HEADS, PAGE, FOX_HEAD_DIM), F32)

    def logf_spec(u):
        return pl.BlockSpec((None, FOX_HEADS, PAGE), lambda i, p, tbl: (tbl[i, n_pages - 1 - (p * pps + u)], 0, 0))

    return pl.pallas_call(
        body,
        grid_spec=pltpu.PrefetchScalarGridSpec(
            num_scalar_prefetch=1,
            grid=(b, n_pages // pps),
            in_specs=[
                per_b((QPAD, width)), hbm, hbm,
                *[logf_spec(u) for u in range(pps)],
                per_b((QPAD, FOX_HEADS)), per_b((FOX_HEADS, PAGE)),
                per_b((PAGE, width)), per_b((PAGE, width)),
                per_b((QPAD, width)),
            ],
            out_specs=per_b((QPAD, width)),
            scratch_shapes=[page_buf, page_buf, pltpu.SemaphoreType.DMA((2,)),
                            pltpu.VMEM((rows, LANES), F32), pltpu.VMEM((rows, LANES), F32),
                            pltpu.VMEM((rows, FOX_HEAD_DIM), F32), pltpu.VMEM((rows, LANES), F32),
                            pltpu.VMEM((FOX_HEADS, LANES), F32)],
        ),
        out_shape=jax.ShapeDtypeStruct((b, QPAD, width), BF16),
        compiler_params=_params(("arbitrary", "arbitrary")),
        name="fox_decode_attn",
    )(table, q16, cache_k, cache_v, *([logf_t] * pps), ln16, ln_t, k_new, v_new, z16)


def _conv_body(x_ref, init_ref, w_ref, b_ref, o_ref, carry_ref):
    @pl.when(pl.program_id(2) == 0)
    def _():
        carry_ref[...] = init_ref[...]

    x = x_ref[...].astype(F32)
    o_ref[...] = _causal_conv_silu(x, carry_ref[...], w_ref, b_ref).astype(o_ref.dtype)
    carry_ref[...] = x[x.shape[0] - SUBLANES:, :]


def _conv_silu(x, init, w, bias, batch, t, tt, tc=2048):
    c = x.shape[1]
    nt = t // tt
    return pl.pallas_call(
        _conv_body,
        grid=(batch, c // tc, nt),
        in_specs=[
            pl.BlockSpec((tt, tc), lambda b, j, i: (b * nt + i, j)),
            pl.BlockSpec((None, SUBLANES, tc), lambda b, j, i: (b, 0, j)),
            pl.BlockSpec((SSD_CONV, tc), lambda b, j, i: (0, j)),
            pl.BlockSpec((1, tc), lambda b, j, i: (0, j)),
        ],
        out_specs=pl.BlockSpec((tt, tc), lambda b, j, i: (b * nt + i, j)),
        out_shape=jax.ShapeDtypeStruct(x.shape, BF16),
        scratch_shapes=[pltpu.VMEM((SUBLANES, tc), F32)],
        compiler_params=_params(("arbitrary", "arbitrary", "arbitrary")),
        name="ssd_conv_silu",
    )(x, init, w, bias.reshape(1, c))


SSD_GROUPS_PER_STEP = 4


def _ssd_body(*refs, valid, has_init):
    if has_init:
        (x_ref, b_ref, c_ref, dt_ref, dtt_ref, z_ref, bias_ref, biast_ref, alog_ref, alogt_ref,
         d_ref, nw_ref, s0_ref, y_ref, sout_ref, st_ref, yacc_ref) = refs
    else:
        (x_ref, b_ref, c_ref, dt_ref, dtt_ref, z_ref, bias_ref, biast_ref, alog_ref, alogt_ref,
         d_ref, nw_ref, y_ref, sout_ref, st_ref, yacc_ref) = refs
        s0_ref = None
    ci = pl.program_id(2)
    L = SSD_CHUNK
    gw = SSD_GROUP_WIDTH
    npairs = SSD_HEADS_PER_GROUP // 2
    causal = lax.broadcasted_iota(jnp.int32, (L, L), 0) >= lax.broadcasted_iota(jnp.int32, (L, L), 1)
    first_head = lax.broadcasted_iota(jnp.int32, (L, LANES), 1) < SSD_HEAD_DIM
    first_head_row = lax.broadcasted_iota(jnp.int32, (1, LANES), 1) < SSD_HEAD_DIM

    for u in range(SSD_GROUPS_PER_STEP):
        c0 = u * gw

        @pl.when(ci == 0)
        def _():
            if has_init:
                for k in range(npairs):
                    st_ref[u, :, k * LANES:(k + 1) * LANES] = s0_ref[c0 + k * LANES:c0 + (k + 1) * LANES, :].T
            else:
                st_ref[u] = jnp.zeros(st_ref.shape[1:], F32)

        dt = _softplus(dt_ref[u] + bias_ref[u])
        dtt = _softplus(dtt_ref[u] + biast_ref[u])
        if valid < L:
            dt = jnp.where(lax.broadcasted_iota(jnp.int32, dt.shape, 0) < valid, dt, 0.0)
            dtt = jnp.where(lax.broadcasted_iota(jnp.int32, dtt.shape, 1) < valid, dtt, 0.0)
        acum = jnp.dot(_tri(L, True), dt * (-jnp.exp(alog_ref[u])),
                       precision=lax.Precision.HIGHEST, preferred_element_type=F32) * LOG2E
        acumt = jnp.dot(dtt * (-jnp.exp(alogt_ref[u])), _tri(L, False),
                        precision=lax.Precision.HIGHEST, preferred_element_type=F32) * LOG2E
        last_t = acumt[:, L - 1:L]
        wt = jnp.exp2(last_t - acumt) * dtt
        chunk_decay = jnp.broadcast_to(jnp.exp2(last_t), (SSD_HEADS_PER_GROUP, LANES))
        src_term = acumt - jnp.log2(dtt)

        bm = b_ref[:, u * SSD_STATE:(u + 1) * SSD_STATE]
        cm = c_ref[:, u * SSD_STATE:(u + 1) * SSD_STATE]
        cb = lax.dot_general(cm, bm, (((1,), (1,)), ((), ())), preferred_element_type=F32)
        cm32 = cm.astype(F32)
        bmt = bm.astype(F32).T

        for k in range(npairs):
            cols = slice(c0 + k * LANES, c0 + (k + 1) * LANES)
            xp = x_ref[:, cols]
            sp = st_ref[u, :, k * LANES:(k + 1) * LANES]
            rhs = jnp.concatenate([xp, sp.astype(BF16)], axis=0)
            ys, ups = [], []
            for r in (2 * k, 2 * k + 1):
                a_col = jnp.broadcast_to(acum[:, r:r + 1], (L, L))
                m = cb * jnp.exp2(jnp.where(causal, a_col - src_term[r:r + 1, :], -jnp.inf))
                ec = cm32 * jnp.exp2(a_col)
                lhs = jnp.concatenate([m, ec], axis=1).astype(BF16)
                ys.append(jnp.dot(lhs, rhs, preferred_element_type=F32))
                bw = (bmt * wt[r:r + 1, :]).astype(BF16)
                ups.append(jnp.dot(bw, xp, preferred_element_type=F32))
            yacc_ref[:, cols] = jnp.where(first_head, ys[0], ys[1])
            dec_row = jnp.where(first_head_row, chunk_decay[2 * k:2 * k + 1, :],
                                chunk_decay[2 * k + 1:2 * k + 2, :])
            st_ref[u, :, k * LANES:(k + 1) * LANES] = sp * dec_row + jnp.where(first_head, ups[0], ups[1])

        gcols = slice(c0, c0 + gw)
        y = yacc_ref[:, gcols] + d_ref[u] * x_ref[:, gcols].astype(F32)
        y = y * _silu(z_ref[:, gcols].astype(F32))
        y = y * lax.rsqrt(jnp.mean(y * y, axis=-1, keepdims=True) + EPS) * nw_ref[u]
        y_ref[:, gcols] = y.astype(y_ref.dtype)

        @pl.when(ci == pl.num_programs(2) - 1)
        def _():
            for k in range(npairs):
                sout_ref[c0 + k * LANES:c0 + (k + 1) * LANES, :] = st_ref[u, :, k * LANES:(k + 1) * LANES].T


def _ssd_scan(act, dt_g, dtt_g, z, dt_bias, a_log, d_skip, norm_w, state0, batch, t, valid):
    L = SSD_CHUNK
    nc = t // L
    gps = SSD_GROUPS_PER_STEP
    gw = SSD_GROUP_WIDTH
    hpg = SSD_HEADS_PER_GROUP
    g = SSD_GROUPS
    inner = g * gw
    b_blk0 = inner // (gps * SSD_STATE)
    c_blk0 = b_blk0 + g // gps
    has_init = state0 is not None
    body = functools.partial(_ssd_body, valid=valid, has_init=has_init)
    grp = lambda shape: pl.BlockSpec((gps,) + shape, lambda b, gi, c: (gi, 0, 0))
    in_specs = [
        pl.BlockSpec((L, gps * gw), lambda b, gi, c: (b * nc + c, gi)),
        pl.BlockSpec((L, gps * SSD_STATE), lambda b, gi, c: (b * nc + c, b_blk0 + gi)),
        pl.BlockSpec((L, gps * SSD_STATE), lambda b, gi, c: (b * nc + c, c_blk0 + gi)),
        pl.BlockSpec((gps, L, hpg), lambda b, gi, c: (gi, b * nc + c, 0)),
        pl.BlockSpec((gps, hpg, L), lambda b, gi, c: (gi, 0, b * nc + c)),
        pl.BlockSpec((L, gps * gw), lambda b, gi, c: (b * nc + c, gi)),
        grp((1, hpg)), grp((hpg, 1)), grp((1, hpg)), grp((hpg, 1)),
        grp((1, gw)), grp((1, gw)),
    ]
    args = [act, act, act, dt_g, dtt_g, z,
            dt_bias.reshape(g, 1, hpg), dt_bias.reshape(g, hpg, 1),
            a_log.reshape(g, 1, hpg), a_log.reshape(g, hpg, 1),
            jnp.repeat(d_skip, SSD_HEAD_DIM).reshape(g, 1, gw), norm_w.reshape(g, 1, gw)]
    state_spec = pl.BlockSpec((None, gps * gw, SSD_STATE), lambda b, gi, c: (b, gi, 0))
    if has_init:
        in_specs.append(state_spec)
        args.append(state0.reshape(batch, g * gw, SSD_STATE))
    y, s_out = pl.pallas_call(
        body,
        grid=(batch, g // gps, nc),
        in_specs=in_specs,
        out_specs=[pl.BlockSpec((L, gps * gw), lambda b, gi, c: (b * nc + c, gi)), state_spec],
        out_shape=[jax.ShapeDtypeStruct((batch * t, inner), BF16),
                   jax.ShapeDtypeStruct((batch, g * gw, SSD_STATE), F32)],
        scratch_shapes=[pltpu.VMEM((gps, SSD_STATE, gw), F32), pltpu.VMEM((L, gps * gw), F32)],
        compiler_params=_params(("arbitrary", "arbitrary", "arbitrary")),
        name="ssd_scan",
    )(*args)
    return y, s_out.reshape(batch, g * hpg, SSD_HEAD_DIM, SSD_STATE)


def _mod_parts(mod_l, nb_prompt, nb_sample, rows_per_sample, d):
    def prompt(a):
        return a[:nb_prompt].reshape(nb_prompt, 1, d)

    def sample(a):
        return jnp.repeat(a[nb_prompt:nb_prompt + nb_sample], rows_per_sample, axis=0)

    parts = [mod_l[:, i * d:(i + 1) * d] for i in range(3)]
    return [prompt(a) for a in parts], [sample(a) for a in parts]


def _pad_rows(a, batch, t, t_pad):
    c = a.shape[-1]
    a = a.reshape(batch, t, c)
    return jnp.pad(a, ((0, 0), (0, t_pad - t), (0, 0))).reshape(batch * t_pad, c)


def kernel(x_prompt, x_sample, c_prompt, c_sample, cache_k, cache_v, cache_logf, state_ssm, state_conv,
           page_table, norm_w, ada_w, ada_b, fox_w_in, fox_b_f, fox_w_out, ssd_w_in, ssd_conv_w, ssd_conv_b,
           ssd_dt_bias, ssd_a_log, ssd_d, ssd_norm_w, ssd_w_out, final_norm_w):
    bp, t, d = x_prompt.shape
    bs, ts, _ = x_sample.shape
    depth = norm_w.shape[0]
    n_fox = fox_w_in.shape[0]
    mp, ms = bp * t, bs * ts
    width = FOX_HEADS * FOX_HEAD_DIM
    inner = SSD_GROUPS * SSD_GROUP_WIDTH
    conv_dim = inner + 2 * SSD_GROUPS * SSD_STATE
    n_heads = SSD_GROUPS * SSD_HEADS_PER_GROUP

    xp = x_prompt.reshape(mp, d)
    xs = x_sample.reshape(ms, d)

    c_all = jnp.concatenate([c_prompt, c_sample], axis=0)
    c_all = jnp.pad(c_all, ((0, (-c_all.shape[0]) % SUBLANES), (0, 0)))
    mod = _ada_mod(c_all, ada_w, ada_b)

    fox_w_in16 = fox_w_in.astype(BF16)
    fox_w_out16 = fox_w_out.astype(BF16)
    ssd_w_in16 = ssd_w_in.astype(BF16)
    ssd_w_out16 = ssd_w_out.astype(BF16)
    fox_w_f16 = jnp.pad(fox_w_in[:, :, 4 * width:], ((0, 0), (0, 0), (0, LANES - FOX_HEADS))).astype(BF16)

    kp_all = jnp.zeros((n_fox, mp, width), F32)
    vp_all = jnp.zeros((n_fox, mp, width), F32)
    outs ={name: [] for name in ("fp", "ks", "vs", "fs", "sp", "cp", "ss", "cs")}
    for i in range(depth):
        j = i // 2
        (sh_p, sc_p, g_p), (sh_s, sc_s, g_s) = _mod_parts(mod[i], bp, bs, ts, d)
        hp = _norm_mod(xp, norm_w[i], sc_p, sh_p, t, 256)
        hs = _norm_mod(xs, norm_w[i], sc_s, sh_s, ts, ms)
        if i % 2 == 0:
            b_f = jnp.pad(fox_b_f[j], (0, LANES - FOX_HEADS)).reshape(1, LANES)

            def logf_of(h):
                return _mm_logsig(h, fox_w_f16[j:j + 1], b_f, "fox_logf")[:, :FOX_HEADS]

            (q,) = _mm_plain(hp, fox_w_in16, j, 0, width, [BF16], "fox_q")
            kp_all, k16 = _mm_kv(hp, fox_w_in16, j, width, width, kp_all, "fox_k")
            vp_all, v16 = _mm_kv(hp, fox_w_in16, j, 2 * width, width, vp_all, "fox_v")
            (z,) = _mm_plain(hp, fox_w_in16, j, 3 * width, width, [BF16], "fox_z")
            logf = logf_of(hp)
            f_cum = _time_cumsum(logf.reshape(bp, t, FOX_HEADS))
            og = _fox_prompt_attn(q, k16, v16, z, jnp.swapaxes(f_cum, 1, 2), bp, t, tq=min(512, t), tk=min(512, t))
            xp = _mm_resid(og, fox_w_out16, j, xp, g_p, t, "fox_out")
            outs["fp"].append(logf.reshape(bp, t, FOX_HEADS))

            (q,) = _mm_plain(hs, fox_w_in16, j, 0, width, [BF16], "fox_q_s")
            k32, k16 = _mm_plain(hs, fox_w_in16, j, width, width, [F32, BF16], "fox_k_s")
            v32, v16 = _mm_plain(hs, fox_w_in16, j, 2 * width, width, [F32, BF16], "fox_v_s")
            (z,) = _mm_plain(hs, fox_w_in16, j, 3 * width, width, [BF16], "fox_z_s")
            logf = logf_of(hs)

            def pad_q(a, rows):
                return jnp.pad(a.reshape(bs, ts, -1), ((0, 0), (0, rows - ts), (0, 0)))

            ln16 = pad_q(logf, QPAD)
            ln_t = jnp.swapaxes(pad_q(logf, PAGE), 1, 2)
            og = _fox_decode_attn(page_table, j, pad_q(q, QPAD), cache_k, cache_v,
                                  jnp.swapaxes(cache_logf[j], 1, 2), ln16, ln_t,
                                  pad_q(k16, PAGE), pad_q(v16, PAGE), pad_q(z, QPAD), ts)
            og = og[:, :ts].reshape(ms, width)
            xs = _mm_resid(og, fox_w_out16, j, xs, g_s, ts, "fox_out_s")
            outs["ks"].append(k32.reshape(bs, ts, FOX_HEADS, FOX_HEAD_DIM))
            outs["vs"].append(v32.reshape(bs, ts, FOX_HEADS, FOX_HEAD_DIM))
            outs["fs"].append(logf.reshape(bs, ts, FOX_HEADS))
        else:
            def mixer(h, x_res, gate, batch, tt_, conv_prev, state0, suffix):
                m = batch * tt_
                (zz,) = _mm_plain(h, ssd_w_in16, j, 0, inner, [BF16], "ssd_z" + suffix)
                (dt_raw,) = _mm_plain(h, ssd_w_in16, j, inner + conv_dim, n_heads, [F32], "ssd_dt" + suffix)
                t_pad = -(-tt_ // SSD_CHUNK) * SSD_CHUNK
                if conv_prev is None:
                    assert t_pad == tt_
                    act, tail = _mm_conv(h, ssd_w_in16, j, inner, conv_dim, tt_, ssd_conv_w[j], ssd_conv_b[j],
                                         "ssd_xbc_conv" + suffix)
                    new_conv = tail[:, SUBLANES - (SSD_CONV - 1):]
                else:
                    (xbc,) = _mm_plain(h, ssd_w_in16, j, inner, conv_dim, [F32], "ssd_xbc" + suffix)
                    init = jnp.pad(conv_prev, ((0, 0), (SUBLANES - (SSD_CONV - 1), 0), (0, 0)))
                    full = jnp.concatenate([conv_prev, xbc.reshape(batch, tt_, conv_dim)], axis=1)
                    new_conv = full[:, tt_:]
                    if t_pad != tt_:
                        xbc = _pad_rows(xbc, batch, tt_, t_pad)
                    act = _conv_silu(xbc, init, ssd_conv_w[j], ssd_conv_b[j], batch, t_pad, min(512, t_pad))
                if t_pad != tt_:
                    zz, dt_raw = (_pad_rows(a, batch, tt_, t_pad) for a in (zz, dt_raw))
                dt_g = dt_raw.reshape(batch * t_pad, SSD_GROUPS, SSD_HEADS_PER_GROUP).transpose(1, 0, 2)
                dtt_g = dt_g.transpose(0, 2, 1)
                y, new_state = _ssd_scan(act, dt_g, dtt_g, zz, ssd_dt_bias[j], ssd_a_log[j], ssd_d[j],
                                         ssd_norm_w[j], state0, batch, t_pad, min(tt_, SSD_CHUNK))
                if t_pad != tt_:
                    y = y.reshape(batch, t_pad, inner)[:, :tt_].reshape(m, inner)
                x_new = _mm_resid(y, ssd_w_out16, j, x_res, gate, tt_, "ssd_out" + suffix)
                return x_new, new_conv, new_state

            xp, cvp, smp = mixer(hp, xp, g_p, bp, t, None, None, "")
            xs, cvs, sms = mixer(hs, xs, g_s, bs, ts, state_conv[j], state_ssm[j], "_s")
            outs["sp"].append(smp)
            outs["cp"].append(cvp)
            outs["ss"].append(sms)
            outs["cs"].append(cvs)

    y_prompt = _final_norm(xp, final_norm_w, 256).reshape(bp, t, d)
    y_sample = _final_norm(xs, final_norm_w, ms).reshape(bs, ts, d)
    kv_shape = (n_fox, bp, t, FOX_HEADS, FOX_HEAD_DIM)
    return (y_prompt, y_sample,
            kp_all.reshape(kv_shape), vp_all.reshape(kv_shape), jnp.stack(outs["fp"]),
            jnp.stack(outs["ks"]), jnp.stack(outs["vs"]), jnp.stack(outs["fs"]),
            jnp.stack(outs["sp"]), jnp.stack(outs["cp"]), jnp.stack(outs["ss"]), jnp.stack(outs["cs"]))
```
